```python
import jax, jax.numpy as jnp
from jax import lax
import numpy as np

D_MODEL = 1024
BATCH = 2
SEQ = 8192
DEPTH = 1

CHUNK = 64
Q_BLOCK = 128
SB_HEAD_DIM = 64
SB_HEADS = (D_MODEL // 2) // SB_HEAD_DIM
SB_WIDTH = SB_HEADS * SB_HEAD_DIM
GLA_HEADS = 4
GLA_DV = (D_MODEL // 2) // GLA_HEADS
GLA_DK = GLA_DV // 2
GLA_VWIDTH = GLA_HEADS * GLA_DV
GLA_KWIDTH = GLA_HEADS * GLA_DK
GLA_GATE_RANK = 16
GLA_TAU = 16.0
MIX_WIDTH = SB_WIDTH + GLA_VWIDTH
D_FF = 2816
CONV_WIDTH = 3
N_MOD = 6
EPS = 1e-6
IN_SPLITS = (SB_WIDTH, SB_WIDTH, SB_WIDTH,
             GLA_KWIDTH, GLA_KWIDTH, GLA_VWIDTH,
             GLA_VWIDTH, GLA_GATE_RANK)
IN_WIDTH = sum(IN_SPLITS)

kernel_name = "hybrid_stickbreak_gla_convffn_adaln"


def rms_norm(x, gain):
    xf = x.astype(jnp.float32)
    n = xf * lax.rsqrt(jnp.mean(xf * xf, axis=-1, keepdims=True) + EPS)
    return (n * gain.astype(jnp.float32)).astype(x.dtype)


def modulate(h, shift, scale):
    return h * (1 + scale[:, None, :]) + shift[:, None, :]


def stick_breaking_attention(q, k, v):
    B, S, H, d = q.shape
    q = q.transpose(0, 2, 1, 3)
    k = k.transpose(0, 2, 1, 3)
    v = v.transpose(0, 2, 1, 3)
    scale = d ** -0.5
    outs = []
    for i in range(S // Q_BLOCK):
        start = i * Q_BLOCK
        end = start + Q_BLOCK
        qb = q[:, :, start:end]
        kb = k[:, :, :end]
        vb = v[:, :, :end]
        z = jnp.einsum('bhqd,bhkd->bhqk', qb, kb).astype(jnp.float32) * scale
        t_pos = start + jnp.arange(Q_BLOCK)
        s_pos = jnp.arange(end)
        mask = s_pos[None, :] < t_pos[:, None]
        log_one_minus = jnp.where(mask, jax.nn.log_sigmoid(-z), 0.0)
        shifted = jnp.concatenate(
            [log_one_minus[..., 1:], jnp.zeros_like(log_one_minus[..., :1])], axis=-1)
        between = lax.cumsum(shifted, axis=3, reverse=True)
        weights = jnp.where(mask, jnp.exp(jax.nn.log_sigmoid(z) + between), 0.0)
        outs.append(jnp.einsum('bhqk,bhkd->bhqd', weights.astype(vb.dtype), vb))
    o = jnp.concatenate(outs, axis=2)
    return o.transpose(0, 2, 1, 3).reshape(B, S, H * d)


def gla_chunk_causal(q, k, v, log_alpha):
    B, S, H, dk = q.shape
    dv = v.shape[-1]
    nc = S // CHUNK
    qc = q.reshape(B, nc, CHUNK, H, dk).astype(jnp.float32) * (dk ** -0.5)
    kc = k.reshape(B, nc, CHUNK, H, dk).astype(jnp.float32)
    vc = v.reshape(B, nc, CHUNK, H, dv).astype(jnp.float32)
    la = log_alpha.reshape(B, nc, CHUNK, H, dk).astype(jnp.float32)
    cum = jnp.cumsum(la, axis=2)
    total = cum[:, :, -1]
    k_dec = kc * jnp.exp(total[:, :, None] - cum)
    chunk_kv = jnp.einsum('bnchk,bnchv->nbhkv', k_dec, vc)
    chunk_decay = jnp.exp(total).transpose(1, 0, 2, 3)

    def step(state, inp):
        decay, kv = inp
        state = decay[..., None] * state + kv
        return state, state

    s0 = jnp.zeros((B, H, dk, dv), jnp.float32)
    _, states = lax.scan(step, s0, (chunk_decay, chunk_kv))
    o = jnp.einsum('bnchk,nbhkv->bnchv', qc, states)
    return o.reshape(B, S, H * dv)


def causal_depthwise_conv(u, w, b):
    S = u.shape[1]
    up = jnp.pad(u, ((0, 0), (CONV_WIDTH - 1, 0), (0, 0)))
    out = b
    for j in range(CONV_WIDTH):
        out = out + w[j] * up[:, j:j + S]
    return out


def setup_inputs(seed: int = 0) -> dict:
    key = jax.random.key(seed)
    ks = jax.random.split(key, 18)

    def nrm(k, shape, scale):
        return jax.random.normal(k, shape, jnp.float32) * scale

    L, D = DEPTH, D_MODEL
    return {
        "x": nrm(ks[0], (BATCH, SEQ, D), 1.0),
        "c": nrm(ks[1], (BATCH, D), 1.0),
        "w_ada": nrm(ks[2], (L, D, N_MOD * D), 0.5 * D ** -0.5),
        "b_ada": nrm(ks[3], (L, N_MOD * D), 0.01),
        "g_norm1": 1.0 + nrm(ks[4], (L, D), 0.02),
        "w_in": nrm(ks[5], (L, D, IN_WIDTH), D ** -0.5),
        "w_fg2": nrm(ks[6], (L, GLA_GATE_RANK, GLA_KWIDTH), GLA_GATE_RANK ** -0.5),
        "b_fg2": nrm(ks[7], (L, GLA_KWIDTH), 0.1),
        "g_gla_out": 1.0 + nrm(ks[8], (L, GLA_VWIDTH), 0.02),
        "w_out": nrm(ks[9], (L, MIX_WIDTH, D), MIX_WIDTH ** -0.5),
        "g_norm2": 1.0 + nrm(ks[10], (L, D), 0.02),
        "w_up": nrm(ks[11], (L, D, 2 * D_FF), D ** -0.5),
        "w_conv": nrm(ks[12], (L, CONV_WIDTH, 2 * D_FF), CONV_WIDTH ** -0.5),
        "b_conv": nrm(ks[13], (L, 2 * D_FF), 0.01),
        "w_down": nrm(ks[14], (L, D_FF, D), D_FF ** -0.5),
        "g_final": 1.0 + nrm(ks[15], (D,), 0.02),
    }


def reference(x, c, w_ada, b_ada, g_norm1, w_in, w_fg2, b_fg2, g_gla_out, w_out,
              g_norm2, w_up, w_conv, b_conv, w_down, g_final):
    B, S, _ = x.shape
    offsets = np.cumsum(IN_SPLITS)[:-1].tolist()
    for l in range(DEPTH):
        mod = jax.nn.silu(c) @ w_ada[l] + b_ada[l]
        shift1, scale1, gate1, shift2, scale2, gate2 = jnp.split(mod, N_MOD, axis=-1)

        h = modulate(rms_norm(x, g_norm1[l]), shift1, scale1)
        proj = h @ w_in[l]
        sb_q, sb_k, sb_v, gq, gk, gv, gg, gf = jnp.split(proj, offsets, axis=-1)

        o_sb = stick_breaking_attention(
            sb_q.reshape(B, S, SB_HEADS, SB_HEAD_DIM),
            sb_k.reshape(B, S, SB_HEADS, SB_HEAD_DIM),
            sb_v.reshape(B, S, SB_HEADS, SB_HEAD_DIM))

        log_alpha = jax.nn.log_sigmoid(
            (gf @ w_fg2[l] + b_fg2[l]).astype(jnp.float32)) / GLA_TAU
        o_gla = gla_chunk_causal(
            gq.reshape(B, S, GLA_HEADS, GLA_DK),
            gk.reshape(B, S, GLA_HEADS, GLA_DK),
            gv.reshape(B, S, GLA_HEADS, GLA_DV),
            log_alpha.reshape(B, S, GLA_HEADS, GLA_DK))
        oh = o_gla.reshape(B, S, GLA_HEADS, GLA_DV)
        oh = oh * lax.rsqrt(jnp.mean(oh * oh, axis=-1, keepdims=True) + EPS)
        o_gla = (oh.reshape(B, S, GLA_VWIDTH) * g_gla_out[l].astype(jnp.float32)
                 ).astype(x.dtype) * jax.nn.silu(gg)

        mixed = jnp.concatenate([o_sb.astype(x.dtype), o_gla], axis=-1) @ w_out[l]
        x = x + (1 + gate1[:, None, :]) * mixed

        h2 = modulate(rms_norm(x, g_norm2[l]), shift2, scale2)
        u = causal_depthwise_conv(h2 @ w_up[l], w_conv[l], b_conv[l])
        val, gte = jnp.split(u, 2, axis=-1)
        x = x + (1 + gate2[:, None, :]) * ((val * jax.nn.silu(gte)) @ w_down[l])
    return rms_norm(x, g_final)
```

```python
import functools

import jax
import jax.numpy as jnp
from jax import lax
from jax.experimental import pallas as pl
from jax.experimental.pallas import tpu as pltpu

F32 = jnp.float32
BF16 = jnp.bfloat16

EPS = 1e-6
SB_HEAD_DIM = 64
SB_WIDTH = 512
GLA_HEADS = 4
GLA_DK = 64
GLA_DV = 128
GLA_KWIDTH = GLA_HEADS * GLA_DK
GLA_VWIDTH = GLA_HEADS * GLA_DV
GLA_GATE_RANK = 16
GLA_TAU = 16.0
CHUNK = 64
CONV_WIDTH = 3
N_MOD = 6

LANES = 128
SUBLANES = 8
VMEM_LIMIT = 56 * 1024 * 1024

ROW_TILE = 512
SB_BLOCK = 128
SB_LOG_ZERO = -110.0
FFN_CHUNK = 256


def _dot(a, b):
    return jnp.dot(a, b, preferred_element_type=F32)


def _dot_nt(a, b):
    return lax.dot_general(a, b, (((1,), (1,)), ((), ())), preferred_element_type=F32)


def _dot_tn(a, b):
    return lax.dot_general(a, b, (((0,), (0,)), ((), ())), preferred_element_type=F32)


def _split_bf16(a):
    hi = a.astype(BF16)
    lo = (a - hi.astype(F32)).astype(BF16)
    return hi, lo


def _rms_normalise(x, gain):
    ms = jnp.mean(x * x, axis=-1, keepdims=True)
    return x * lax.rsqrt(ms + EPS) * gain


def _adaln_kernel(c_ref, w_ref, b_ref, o_ref):
    c = c_ref[...]
    a = c * (1.0 / (1.0 + jnp.exp(-c)))
    a_hi, a_lo = _split_bf16(a)
    w_hi, w_lo = _split_bf16(w_ref[...])
    o_ref[...] = (_dot(a_hi, w_hi) + _dot(a_lo, w_hi) + _dot(a_hi, w_lo)) + b_ref[...]


def _adaln(c_pad, w_ada, b_ada):
    rows, d = c_pad.shape
    n = w_ada.shape[1]
    tn = 1024
    return pl.pallas_call(
        _adaln_kernel,
        out_shape=jax.ShapeDtypeStruct((rows, n), F32),
        grid=(n // tn,),
        in_specs=[
            pl.BlockSpec((rows, d), lambda j: (0, 0)),
            pl.BlockSpec((d, tn), lambda j: (0, j)),
            pl.BlockSpec((1, tn), lambda j: (0, j)),
        ],
        out_specs=pl.BlockSpec((rows, tn), lambda j: (0, j)),
        compiler_params=pltpu.CompilerParams(
            dimension_semantics=("arbitrary",), vmem_limit_bytes=VMEM_LIMIT),
        name="adaln",
    )(c_pad, w_ada, b_ada)


def _inproj_kernel(x_ref, mod_ref, g_ref, wsb_ref, wgla_ref, wfg_ref, bfg_ref,
                   sbq_ref, sbk_ref, sbv_ref, gq_ref, gk_ref, gv_ref, gg_ref, la_ref):
    x = x_ref[0]
    shift = mod_ref[0, 0:1, :]
    scale = mod_ref[0, 1:2, :]
    h = _rms_normalise(x, g_ref[...]) * (1.0 + scale) + shift
    hb = h.astype(BF16)

    w = SB_WIDTH
    sbq_ref[0] = (_dot(hb, wsb_ref[:, 0:w]) * (SB_HEAD_DIM ** -0.5)).astype(BF16)
    sbk_ref[0] = _dot(hb, wsb_ref[:, w:2 * w]).astype(BF16)
    sbv_ref[0] = _dot(hb, wsb_ref[:, 2 * w:3 * w]).astype(BF16)

    kw, vw = GLA_KWIDTH, GLA_VWIDTH
    gq_ref[0] = (_dot(hb, wgla_ref[:, 0:kw]) * (GLA_DK ** -0.5)).astype(BF16)
    gk_ref[0] = _dot(hb, wgla_ref[:, kw:2 * kw]).astype(BF16)
    o = 2 * kw
    gv_ref[0] = _dot(hb, wgla_ref[:, o:o + vw]).astype(BF16)
    gg_ref[0] = _dot(hb, wgla_ref[:, o + vw:o + 2 * vw]).astype(BF16)
    gf = _dot(hb, wgla_ref[:, o + 2 * vw:o + 2 * vw + LANES])
    gf_hi, gf_lo = _split_bf16(gf)
    wfg_hi, wfg_lo = _split_bf16(wfg_ref[...])
    pre = (_dot(gf_hi, wfg_hi) + _dot(gf_lo, wfg_hi) + _dot(gf_hi, wfg_lo)) + bfg_ref[...]
    log_sig = jnp.minimum(pre, 0.0) - jnp.log(1.0 + jnp.exp(-jnp.abs(pre)))
    la_ref[0] = log_sig * (1.0 / GLA_TAU)


def _inproj(x, mod, g1, w_sb, w_gla, w_fg, b_fg):
    b, s, d = x.shape
    tm = ROW_TILE
    const = lambda bi, i: (0, 0)
    row = lambda bi, i: (bi, i, 0)
    widths = (SB_WIDTH, SB_WIDTH, SB_WIDTH, GLA_KWIDTH, GLA_KWIDTH, GLA_VWIDTH, GLA_VWIDTH)
    out_shape = [jax.ShapeDtypeStruct((b, s, wd), BF16) for wd in widths]
    out_shape.append(jax.ShapeDtypeStruct((b, s, GLA_KWIDTH), F32))
    out_specs = [pl.BlockSpec((1, tm, wd), row) for wd in widths]
    out_specs.append(pl.BlockSpec((1, tm, GLA_KWIDTH), row))
    return pl.pallas_call(
        _inproj_kernel,
        out_shape=out_shape,
        grid=(b, s // tm),
        in_specs=[
            pl.BlockSpec((1, tm, d), row),
            pl.BlockSpec((1, N_MOD, d), lambda bi, i: (bi, 0, 0)),
            pl.BlockSpec((1, d), const),
            pl.BlockSpec(w_sb.shape, const, pipeline_mode=pl.Buffered(1)),
            pl.BlockSpec(w_gla.shape, const, pipeline_mode=pl.Buffered(1)),
            pl.BlockSpec(w_fg.shape, const),
            pl.BlockSpec(b_fg.shape, const),
        ],
        out_specs=out_specs,
        compiler_params=pltpu.CompilerParams(
            dimension_semantics=("arbitrary", "arbitrary"), vmem_limit_bytes=VMEM_LIMIT),
        name="inproj",
    )(x, mod, g1, w_sb, w_gla, w_fg, b_fg)


def _sb_kernel(q_ref, k_ref, v_ref, o_ref, acc_ref, run_ref):
    blk = SB_BLOCK
    i = pl.program_id(2)
    q = q_ref[0]
    lane = lax.broadcasted_iota(jnp.int32, (1, LANES), 1)
    first = lane < SB_HEAD_DIM
    zq = jnp.zeros_like(q)
    q_heads = (jnp.where(first, q, zq), jnp.where(first, zq, q))

    kj = lax.broadcasted_iota(jnp.int32, (blk, 2 * blk), 0)
    ks = lax.broadcasted_iota(jnp.int32, (blk, 2 * blk), 1)
    tri = jnp.where((kj > ks) | (ks >= blk), 1.0, 0.0).astype(BF16)

    t_pos = lax.broadcasted_iota(jnp.int32, (blk, blk), 0)
    s_pos = lax.broadcasted_iota(jnp.int32, (blk, blk), 1)
    causal = s_pos < t_pos

    acc_ref[...] = jnp.zeros_like(acc_ref)
    run_ref[...] = jnp.zeros_like(run_ref)

    def key_block(kb, diagonal):
        start = pl.multiple_of(kb * blk, blk)
        k = k_ref[0, pl.ds(start, blk), :]
        v = v_ref[0, pl.ds(start, blk), :]
        weights = []
        for h in range(2):
            z = _dot_nt(q_heads[h], k)
            log_rest = jnp.minimum(-z, 0.0) - jnp.log(1.0 + jnp.exp(-jnp.abs(z)))
            if diagonal:
                log_rest = jnp.where(causal, log_rest, 0.0)
            hi, lo = _split_bf16(log_rest)
            sums = _dot(hi, tri) + _dot(lo, tri)
            run = run_ref[h]
            wgt = jnp.exp(z + log_rest + sums[:, :blk] + run)
            if diagonal:
                wgt = jnp.where(causal, wgt, 0.0)
            run_ref[h] = run + sums[:, blk:]
            weights.append(wgt.astype(BF16))
        zv = jnp.zeros_like(v)
        v_heads = jnp.concatenate([jnp.where(first, v, zv), jnp.where(first, zv, v)], axis=0)
        acc_ref[...] += _dot(jnp.concatenate(weights, axis=1), v_heads)

    key_block(i, True)

    def cond(state):
        kb, top = state
        return jnp.logical_and(kb >= 0, top > SB_LOG_ZERO)

    def body(state):
        kb, _ = state
        key_block(kb, False)
        top = jnp.max(jnp.maximum(run_ref[0], run_ref[1]))
        return kb - 1, top

    lax.while_loop(cond, body, (i - 1, jnp.float32(0.0)))
    o_ref[0] = acc_ref[...].astype(BF16)


def _sb_attention(q, k, v):
    b, s, w = q.shape
    blk = SB_BLOCK
    pairs = w // LANES
    kv_spec = pl.BlockSpec((1, s, LANES), lambda bi, p, i: (bi, 0, p))
    q_spec = pl.BlockSpec((1, blk, LANES), lambda bi, p, i: (bi, i, p))
    return pl.pallas_call(
        _sb_kernel,
        out_shape=jax.ShapeDtypeStruct((b, s, w), BF16),
        grid=(b, pairs, s // blk),
        in_specs=[q_spec, kv_spec, kv_spec],
        out_specs=q_spec,
        scratch_shapes=[pltpu.VMEM((blk, LANES), F32), pltpu.VMEM((2, blk, blk), F32)],
        compiler_params=pltpu.CompilerParams(
            dimension_semantics=("arbitrary", "arbitrary", "arbitrary"),
            vmem_limit_bytes=VMEM_LIMIT),
        name="sb_attn",
    )(q, k, v)


def _gla_kernel(q_ref, k_ref, v_ref, la_ref, gate_ref, gain_ref, o_ref, state_ref, o_scr):
    tm = q_ref.shape[1]
    n_chunks = tm // CHUNK
    shift = CHUNK.bit_length() - 1

    @pl.when(pl.program_id(1) == 0)
    def _():
        state_ref[...] = jnp.zeros_like(state_ref)

    la_hi, la_lo = _split_bf16(la_ref[0])

    r = lax.broadcasted_iota(jnp.int32, (tm, tm), 0)
    c = lax.broadcasted_iota(jnp.int32, (tm, tm), 1)
    later = jnp.where(((r >> shift) == (c >> shift)) & (c > r), 1.0, 0.0).astype(BF16)
    rev = _dot(later, la_hi) + _dot(later, la_lo)
    k_dec = (k_ref[0].astype(F32) * jnp.exp(rev)).astype(BF16)

    rr = lax.broadcasted_iota(jnp.int32, (tm, n_chunks * LANES), 0)
    cc = lax.broadcasted_iota(jnp.int32, (tm, n_chunks * LANES), 1)
    member = jnp.where((rr >> shift) == (cc >> (LANES.bit_length() - 1)), 1.0, 0.0).astype(BF16)
    total = _dot_tn(la_hi, member) + _dot_tn(la_lo, member)
    decay = jnp.exp(total)

    sr = lax.broadcasted_iota(jnp.int32, state_ref.shape, 0)
    sc = lax.broadcasted_iota(jnp.int32, state_ref.shape, 1)
    own_head = (sr >> (GLA_DK.bit_length() - 1)) == (sc >> (GLA_DV.bit_length() - 1))

    q = q_ref[0]
    v = v_ref[0]
    for ci in range(n_chunks):
        rows = slice(ci * CHUNK, (ci + 1) * CHUNK)
        kv = _dot_tn(k_dec[rows], v[rows])
        d = decay[:, ci * LANES:(ci + 1) * LANES]
        d = jnp.concatenate([d] * (GLA_VWIDTH // LANES), axis=1)
        state = d * state_ref[...] + jnp.where(own_head, kv, 0.0)
        state_ref[...] = state
        o_scr[rows, :] = _dot(q[rows], state.astype(BF16))

    gain = gain_ref[...]
    gate = gate_ref[0].astype(F32)
    swish = gate * (1.0 / (1.0 + jnp.exp(-gate)))
    for h in range(GLA_HEADS):
        cols = slice(h * GLA_DV, (h + 1) * GLA_DV)
        oh = o_scr[:, cols]
        ms = jnp.mean(oh * oh, axis=-1, keepdims=True)
        o_ref[0, :, cols] = (oh * lax.rsqrt(ms + EPS) * gain[:, cols] * swish[:, cols]).astype(BF16)


def _gla(q, k, v, la, gate, gain):
    b, s, _ = q.shape
    tm = ROW_TILE
    row = lambda bi, i: (bi, i, 0)
    return pl.pallas_call(
        _gla_kernel,
        out_shape=jax.ShapeDtypeStruct((b, s, GLA_VWIDTH), BF16),
        grid=(b, s // tm),
        in_specs=[
            pl.BlockSpec((1, tm, GLA_KWIDTH), row),
            pl.BlockSpec((1, tm, GLA_KWIDTH), row),
            pl.BlockSpec((1, tm, GLA_VWIDTH), row),
            pl.BlockSpec((1, tm, GLA_KWIDTH), row),
            pl.BlockSpec((1, tm, GLA_VWIDTH), row),
            pl.BlockSpec((1, GLA_VWIDTH), lambda bi, i: (0, 0)),
        ],
        out_specs=pl.BlockSpec((1, tm, GLA_VWIDTH), row),
        scratch_shapes=[pltpu.VMEM((GLA_KWIDTH, GLA_VWIDTH), F32),
                        pltpu.VMEM((tm, GLA_VWIDTH), F32)],
        compiler_params=pltpu.CompilerParams(
            dimension_semantics=("arbitrary", "arbitrary"), vmem_limit_bytes=VMEM_LIMIT),
        name="gla",
    )(q, k, v, la, gate, gain)


def _outproj_kernel(x_ref, osb_ref, ogla_ref, mod_ref, g_ref, wsb_ref, wgla_ref, x1_ref, h2_ref):
    mixed = _dot(osb_ref[0], wsb_ref[...]) + _dot(ogla_ref[0], wgla_ref[...])
    gate1 = mod_ref[0, 2:3, :]
    shift2 = mod_ref[0, 3:4, :]
    scale2 = mod_ref[0, 4:5, :]
    x1 = x_ref[0] + (1.0 + gate1) * mixed
    x1_ref[0] = x1
    h2_ref[0] = (_rms_normalise(x1, g_ref[...]) * (1.0 + scale2) + shift2).astype(BF16)


def _outproj(x, o_sb, o_gla, mod, g2, w_sb, w_gla):
    b, s, d = x.shape
    tm = ROW_TILE
    const = lambda bi, i: (0, 0)
    row = lambda bi, i: (bi, i, 0)
    return pl.pallas_call(
        _outproj_kernel,
        out_shape=[jax.ShapeDtypeStruct((b, s, d), F32), jax.ShapeDtypeStruct((b, s, d), BF16)],
        grid=(b, s // tm),
        in_specs=[
            pl.BlockSpec((1, tm, d), row),
            pl.BlockSpec((1, tm, SB_WIDTH), row),
            pl.BlockSpec((1, tm, GLA_VWIDTH), row),
            pl.BlockSpec((1, N_MOD, d), lambda bi, i: (bi, 0, 0)),
            pl.BlockSpec((1, d), const),
            pl.BlockSpec(w_sb.shape, const, pipeline_mode=pl.Buffered(1)),
            pl.BlockSpec(w_gla.shape, const, pipeline_mode=pl.Buffered(1)),
        ],
        out_specs=[pl.BlockSpec((1, tm, d), row), pl.BlockSpec((1, tm, d), row)],
        compiler_params=pltpu.CompilerParams(
            dimension_semantics=("arbitrary", "arbitrary"), vmem_limit_bytes=VMEM_LIMIT),
        name="outproj",
    )(x, o_sb, o_gla, mod, g2, w_sb, w_gla)


def _ffn_kernel(h_ref, x1_ref, mod_ref, gfin_ref, wup_ref, wconv_ref, bconv_ref, wdown_ref,
                o_ref, tail_ref):
    tm = h_ref.shape[1]
    d_ff = wdown_ref.shape[0]
    fc = FFN_CHUNK

    @pl.when(pl.program_id(1) == 0)
    def _():
        tail_ref[...] = jnp.zeros_like(tail_ref)

    h = h_ref[0]

    def conv(cols):
        u = _dot(h, wup_ref[:, cols])
        ext = jnp.concatenate([tail_ref[:, cols], u], axis=0)
        tail_ref[:, cols] = u[tm - SUBLANES:, :]
        out = bconv_ref[:, cols] + wconv_ref[CONV_WIDTH - 1:CONV_WIDTH, cols] * u
        for j in range(CONV_WIDTH - 1):
            back = CONV_WIDTH - 1 - j
            out = out + wconv_ref[j:j + 1, cols] * ext[SUBLANES - back:SUBLANES - back + tm, :]
        return out

    acc = jnp.zeros((tm, o_ref.shape[2]), F32)
    for ci in range(d_ff // fc):
        val = conv(slice(ci * fc, (ci + 1) * fc))
        gte = conv(slice(d_ff + ci * fc, d_ff + (ci + 1) * fc))
        act = val * (gte * (1.0 / (1.0 + jnp.exp(-gte))))
        acc = acc + _dot(act.astype(BF16), wdown_ref[ci * fc:(ci + 1) * fc, :])

    gate2 = mod_ref[0, 5:6, :]
    y = x1_ref[0] + (1.0 + gate2) * acc
    o_ref[0] = _rms_normalise(y, gfin_ref[...])


def _ffn(h2, x1, mod, g_final, w_up, w_conv, b_conv, w_down):
    b, s, d = x1.shape
    tm = ROW_TILE
    const = lambda bi, i: (0, 0)
    row = lambda bi, i: (bi, i, 0)
    return pl.pallas_call(
        _ffn_kernel,
        out_shape=jax.ShapeDtypeStruct((b, s, d), F32),
        grid=(b, s // tm),
        in_specs=[
            pl.BlockSpec((1, tm, d), row),
            pl.BlockSpec((1, tm, d), row),
            pl.BlockSpec((1, N_MOD, d), lambda bi, i: (bi, 0, 0)),
            pl.BlockSpec((1, d), const),
            pl.BlockSpec(w_up.shape, const, pipeline_mode=pl.Buffered(1)),
            pl.BlockSpec(w_conv.shape, const),
            pl.BlockSpec(b_conv.shape, const),
            pl.BlockSpec(w_down.shape, const, pipeline_mode=pl.Buffered(1)),
        ],
        out_specs=pl.BlockSpec((1, tm, d), row),
        scratch_shapes=[pltpu.VMEM((SUBLANES, w_up.shape[1]), F32)],
        compiler_params=pltpu.CompilerParams(
            dimension_semantics=("arbitrary", "arbitrary"), vmem_limit_bytes=VMEM_LIMIT),
        name="ffn",
    )(h2, x1, mod, g_final, w_up, w_conv, b_conv, w_down)


def kernel(x, c, w_ada, b_ada, g_norm1, w_in, w_fg2, b_fg2, g_gla_out, w_out,
           g_norm2, w_up, w_conv, b_conv, w_down, g_final):
    depth = w_ada.shape[0]
    bsz, _, d = x.shape
    sb_cols = 3 * SB_WIDTH
    gla_cols = 2 * GLA_KWIDTH + 2 * GLA_VWIDTH
    out = x
    for l in range(depth):
        c_pad = jnp.zeros((SUBLANES, d), F32).at[:bsz].set(c)
        mod = _adaln(c_pad, w_ada[l], b_ada[l][None, :])[:bsz].reshape(bsz, N_MOD, d)

        w_sb = w_in[l][:, :sb_cols].astype(BF16)
        w_gla = jnp.pad(w_in[l][:, sb_cols:], ((0, 0), (0, LANES - GLA_GATE_RANK))).astype(BF16)
        w_fg = jnp.pad(w_fg2[l], ((0, LANES - GLA_GATE_RANK), (0, 0)))
        assert w_gla.shape[1] == gla_cols + LANES
        sbq, sbk, sbv, gq, gk, gv, gg, la = _inproj(
            out, mod, g_norm1[l][None, :], w_sb, w_gla, w_fg, b_fg2[l][None, :])

        o_sb = _sb_attention(sbq, sbk, sbv)
        o_gla = _gla(gq, gk, gv, la, gg, g_gla_out[l][None, :])

        w_o = w_out[l].astype(BF16)
        x1, h2 = _outproj(out, o_sb, o_gla, mod, g_norm2[l][None, :],
                          w_o[:SB_WIDTH], w_o[SB_WIDTH:])
        assert depth == 1
        out = _ffn(h2, x1, mod, g_final[None, :], w_up[l].astype(BF16), w_conv[l],
                   b_conv[l][None, :], w_down[l].astype(BF16))
    return out
```

```python
import functools

import jax
import jax.numpy as jnp
from jax import lax
from jax.experimental import pallas as pl
from jax.experimental.pallas import tpu as pltpu

F32 = jnp.float32
BF16 = jnp.bfloat16

EPS = 1e-6
SB_HEAD_DIM = 64
SB_WIDTH = 512
GLA_HEADS = 4
GLA_DK = 64
GLA_DV = 128
GLA_KWIDTH = GLA_HEADS * GLA_DK
GLA_VWIDTH = GLA_HEADS * GLA_DV
GLA_GATE_RANK = 16
GLA_TAU = 16.0
CHUNK = 64
CONV_WIDTH = 3
N_MOD = 6

LANES = 128
SUBLANES = 8
VMEM_LIMIT = 56 * 1024 * 1024

ROW_TILE = 512
SB_BLOCK = 128
SB_LOG_ZERO = -110.0
FFN_CHUNK = 256


def _dot(a, b):
    return jnp.dot(a, b, preferred_element_type=F32)


def _dot_nt(a, b):
    return lax.dot_general(a, b, (((1,), (1,)), ((), ())), preferred_element_type=F32)


def _dot_tn(a, b):
    return lax.dot_general(a, b, (((0,), (0,)), ((), ())), preferred_element_type=F32)


def _split_bf16(a):
    hi = a.astype(BF16)
    lo = (a - hi.astype(F32)).astype(BF16)
    return hi, lo


def _rms_normalise(x, gain):
    ms = jnp.mean(x * x, axis=-1, keepdims=True)
    return x * lax.rsqrt(ms + EPS) * gain


def _adaln_kernel(c_ref, w_ref, b_ref, o_ref):
    c = c_ref[...]
    a = c * (1.0 / (1.0 + jnp.exp(-c)))
    a_hi, a_lo = _split_bf16(a)
    w_hi, w_lo = _split_bf16(w_ref[...])
    o_ref[...] = (_dot(a_hi, w_hi) + _dot(a_lo, w_hi) + _dot(a_hi, w_lo)) + b_ref[...]


def _adaln(c_pad, w_ada, b_ada):
    rows, d = c_pad.shape
    n = w_ada.shape[1]
    tn = 1024
    return pl.pallas_call(
        _adaln_kernel,
        out_shape=jax.ShapeDtypeStruct((rows, n), F32),
        grid=(n // tn,),
        in_specs=[
            pl.BlockSpec((rows, d), lambda j: (0, 0)),
            pl.BlockSpec((d, tn), lambda j: (0, j)),
            pl.BlockSpec((1, tn), lambda j: (0, j)),
        ],
        out_specs=pl.BlockSpec((rows, tn), lambda j: (0, j)),
        compiler_params=pltpu.CompilerParams(
            dimension_semantics=("arbitrary",), vmem_limit_bytes=VMEM_LIMIT),
        name="adaln",
    )(c_pad, w_ada, b_ada)


def _inproj_kernel(x_ref, mod_ref, g_ref, wsb_ref, wgla_ref, wfg_ref, bfg_ref,
                   sbq_ref, sbk_ref, sbv_ref, gq_ref, gk_ref, gv_ref, gg_ref, la_ref):
    x = x_ref[0]
    shift = mod_ref[0, 0:1, :]
    scale = mod_ref[0, 1:2, :]
    h = _rms_normalise(x, g_ref[...]) * (1.0 + scale) + shift
    hb = h.astype(BF16)

    w = SB_WIDTH
    sbq_ref[0] = (_dot(hb, wsb_ref[:, 0:w]) * (SB_HEAD_DIM ** -0.5)).astype(BF16)
    sbk_ref[0] = _dot(hb, wsb_ref[:, w:2 * w]).astype(BF16)
    sbv_ref[0] = _dot(hb, wsb_ref[:, 2 * w:3 * w]).astype(BF16)

    kw, vw = GLA_KWIDTH, GLA_VWIDTH
    gq_ref[0] = (_dot(hb, wgla_ref[:, 0:kw]) * (GLA_DK ** -0.5)).astype(BF16)
    gk_ref[0] = _dot(hb, wgla_ref[:, kw:2 * kw]).astype(BF16)
    o = 2 * kw
    gv_ref[0] = _dot(hb, wgla_ref[:, o:o + vw]).astype(BF16)
    gg_ref[0] = _dot(hb, wgla_ref[:, o + vw:o + 2 * vw]).astype(BF16)
    gf = _dot(hb, wgla_ref[:, o + 2 * vw:o + 2 * vw + LANES])
    gf_hi, gf_lo = _split_bf16(gf)
    wfg_hi, wfg_lo = _split_bf16(wfg_ref[...])
    pre = (_dot(gf_hi, wfg_hi) + _dot(gf_lo, wfg_hi) + _dot(gf_hi, wfg_lo)) + bfg_ref[...]
    log_sig = jnp.minimum(pre, 0.0) - jnp.log(1.0 + jnp.exp(-jnp.abs(pre)))
    la_ref[0] = log_sig * (1.0 / GLA_TAU)


def _inproj(x, mod, g1, w_sb, w_gla, w_fg, b_fg):
    b, s, d = x.shape
    tm = ROW_TILE
    const = lambda bi, i: (0, 0)
    row = lambda bi, i: (bi, i, 0)
    widths = (SB_WIDTH, SB_WIDTH, SB_WIDTH, GLA_KWIDTH, GLA_KWIDTH, GLA_VWIDTH, GLA_VWIDTH)
    out_shape = [jax.ShapeDtypeStruct((b, s, wd), BF16) for wd in widths]
    out_shape.append(jax.ShapeDtypeStruct((b, s, GLA_KWIDTH), F32))
    out_specs = [pl.BlockSpec((1, tm, wd), row) for wd in widths]
    out_specs.append(pl.BlockSpec((1, tm, GLA_KWIDTH), row))
    return pl.pallas_call(
        _inproj_kernel,
        out_shape=out_shape,
        grid=(b, s // tm),
        in_specs=[
            pl.BlockSpec((1, tm, d), row),
            pl.BlockSpec((1, N_MOD, d), lambda bi, i: (bi, 0, 0)),
            pl.BlockSpec((1, d), const),
            pl.BlockSpec(w_sb.shape, const, pipeline_mode=pl.Buffered(1)),
            pl.BlockSpec(w_gla.shape, const, pipeline_mode=pl.Buffered(1)),
            pl.BlockSpec(w_fg.shape, const),
            pl.BlockSpec(b_fg.shape, const),
        ],
        out_specs=out_specs,
        compiler_params=pltpu.CompilerParams(
            dimension_semantics=("arbitrary", "arbitrary"), vmem_limit_bytes=VMEM_LIMIT),
        name="inproj",
    )(x, mod, g1, w_sb, w_gla, w_fg, b_fg)


def _sb_kernel(q_ref, k_ref, v_ref, o_ref, acc_ref, run_ref):
    blk = SB_BLOCK
    pairs = q_ref.shape[2] // LANES
    i = pl.program_id(1)
    lane = lax.broadcasted_iota(jnp.int32, (1, LANES), 1)
    first = lane < SB_HEAD_DIM
    q_heads = []
    for p in range(pairs):
        q = q_ref[0, :, p * LANES:(p + 1) * LANES]
        zq = jnp.zeros_like(q)
        q_heads += [jnp.where(first, q, zq), jnp.where(first, zq, q)]

    kj = lax.broadcasted_iota(jnp.int32, (blk, 2 * blk), 0)
    ks = lax.broadcasted_iota(jnp.int32, (blk, 2 * blk), 1)
    tri = jnp.where((kj > ks) | (ks >= blk), 1.0, 0.0).astype(BF16)

    t_pos = lax.broadcasted_iota(jnp.int32, (blk, blk), 0)
    s_pos = lax.broadcasted_iota(jnp.int32, (blk, blk), 1)
    causal = s_pos < t_pos

    acc_ref[...] = jnp.zeros_like(acc_ref)
    run_ref[...] = jnp.zeros_like(run_ref)

    def key_block(kb, diagonal):
        start = pl.multiple_of(kb * blk, blk)
        heads = range(2 * pairs)
        scores = []
        for h in heads:
            k = k_ref[0, pl.ds(start, blk), (h // 2) * LANES:(h // 2 + 1) * LANES]
            scores.append(_dot_nt(q_heads[h], k))
        log_keep, log_rest = [], []
        for h in heads:
            z = scores[h]
            rest = jnp.minimum(-z, 0.0) - jnp.log(1.0 + jnp.exp(-jnp.abs(z)))
            if diagonal:
                rest = jnp.where(causal, rest, 0.0)
            log_rest.append(rest)
            log_keep.append(z + rest)
        sums = []
        for h in heads:
            hi, lo = _split_bf16(log_rest[h])
            sums.append(_dot(hi, tri) + _dot(lo, tri))
        weights = []
        for h in heads:
            run = run_ref[h]
            wgt = jnp.exp(log_keep[h] + sums[h][:, :blk] + run)
            if diagonal:
                wgt = jnp.where(causal, wgt, 0.0)
            run_ref[h] = run + sums[h][:, blk:]
            weights.append(wgt.astype(BF16))
        for p in range(pairs):
            cols = slice(p * LANES, (p + 1) * LANES)
            v = v_ref[0, pl.ds(start, blk), cols]
            zv = jnp.zeros_like(v)
            v_heads = jnp.concatenate([jnp.where(first, v, zv), jnp.where(first, zv, v)], axis=0)
            acc_ref[:, cols] += _dot(jnp.concatenate(weights[2 * p:2 * p + 2], axis=1), v_heads)

    key_block(i, True)

    def cond(state):
        kb, top = state
        return jnp.logical_and(kb >= 0, top > SB_LOG_ZERO)

    def body(state):
        kb, _ = state
        key_block(kb, False)
        top = run_ref[0]
        for h in range(1, 2 * pairs):
            top = jnp.maximum(top, run_ref[h])
        return kb - 1, jnp.max(top)

    lax.while_loop(cond, body, (i - 1, jnp.float32(0.0)))
    o_ref[0] = acc_ref[...].astype(BF16)


def _sb_attention(q, k, v):
    b, s, w = q.shape
    blk = SB_BLOCK
    heads = w // SB_HEAD_DIM
    kv_spec = pl.BlockSpec((1, s, w), lambda bi, i: (bi, 0, 0), pipeline_mode=pl.Buffered(1))
    q_spec = pl.BlockSpec((1, blk, w), lambda bi, i: (bi, i, 0))
    return pl.pallas_call(
        _sb_kernel,
        out_shape=jax.ShapeDtypeStruct((b, s, w), BF16),
        grid=(b, s // blk),
        in_specs=[q_spec, kv_spec, kv_spec],
        out_specs=q_spec,
        scratch_shapes=[pltpu.VMEM((blk, w), F32), pltpu.VMEM((heads, blk, blk), F32)],
        compiler_params=pltpu.CompilerParams(
            dimension_semantics=("arbitrary", "arbitrary"),
            vmem_limit_bytes=VMEM_LIMIT),
        name="sb_attn",
    )(q, k, v)


def _gla_kernel(q_ref, k_ref, v_ref, la_ref, gate_ref, gain_ref, o_ref, state_ref, o_scr):
    tm = q_ref.shape[1]
    n_chunks = tm // CHUNK
    shift = CHUNK.bit_length() - 1

    @pl.when(pl.program_id(1) == 0)
    def _():
        state_ref[...] = jnp.zeros_like(state_ref)

    la_hi, la_lo = _split_bf16(la_ref[0])

    r = lax.broadcasted_iota(jnp.int32, (tm, tm), 0)
    c = lax.broadcasted_iota(jnp.int32, (tm, tm), 1)
    later = jnp.where(((r >> shift) == (c >> shift)) & (c > r), 1.0, 0.0).astype(BF16)
    rev = _dot(later, la_hi) + _dot(later, la_lo)
    k_dec = (k_ref[0].astype(F32) * jnp.exp(rev)).astype(BF16)

    rr = lax.broadcasted_iota(jnp.int32, (tm, n_chunks * LANES), 0)
    cc = lax.broadcasted_iota(jnp.int32, (tm, n_chunks * LANES), 1)
    member = jnp.where((rr >> shift) == (cc >> (LANES.bit_length() - 1)), 1.0, 0.0).astype(BF16)
    total = _dot_tn(la_hi, member) + _dot_tn(la_lo, member)
    decay = jnp.exp(total)

    sr = lax.broadcasted_iota(jnp.int32, state_ref.shape, 0)
    sc = lax.broadcasted_iota(jnp.int32, state_ref.shape, 1)
    own_head = (sr >> (GLA_DK.bit_length() - 1)) == (sc >> (GLA_DV.bit_length() - 1))

    q = q_ref[0]
    v = v_ref[0]
    for ci in range(n_chunks):
        rows = slice(ci * CHUNK, (ci + 1) * CHUNK)
        kv = _dot_tn(k_dec[rows], v[rows])
        d = decay[:, ci * LANES:(ci + 1) * LANES]
        d = jnp.concatenate([d] * (GLA_VWIDTH // LANES), axis=1)
        state = d * state_ref[...] + jnp.where(own_head, kv, 0.0)
        state_ref[...] = state
        o_scr[rows, :] = _dot(q[rows], state.astype(BF16))

    gain = gain_ref[...]
    gate = gate_ref[0].astype(F32)
    swish = gate * (1.0 / (1.0 + jnp.exp(-gate)))
    for h in range(GLA_HEADS):
        cols = slice(h * GLA_DV, (h + 1) * GLA_DV)
        oh = o_scr[:, cols]
        ms = jnp.mean(oh * oh, axis=-1, keepdims=True)
        o_ref[0, :, cols] = (oh * lax.rsqrt(ms + EPS) * gain[:, cols] * swish[:, cols]).astype(BF16)


def _gla(q, k, v, la, gate, gain):
    b, s, _ = q.shape
    tm = ROW_TILE
    row = lambda bi, i: (bi, i, 0)
    return pl.pallas_call(
        _gla_kernel,
        out_shape=jax.ShapeDtypeStruct((b, s, GLA_VWIDTH), BF16),
        grid=(b, s // tm),
        in_specs=[
            pl.BlockSpec((1, tm, GLA_KWIDTH), row),
            pl.BlockSpec((1, tm, GLA_KWIDTH), row),
            pl.BlockSpec((1, tm, GLA_VWIDTH), row),
            pl.BlockSpec((1, tm, GLA_KWIDTH), row),
            pl.BlockSpec((1, tm, GLA_VWIDTH), row),
            pl.BlockSpec((1, GLA_VWIDTH), lambda bi, i: (0, 0)),
        ],
        out_specs=pl.BlockSpec((1, tm, GLA_VWIDTH), row),
        scratch_shapes=[pltpu.VMEM((GLA_KWIDTH, GLA_VWIDTH), F32),
                        pltpu.VMEM((tm, GLA_VWIDTH), F32)],
        compiler_params=pltpu.CompilerParams(
            dimension_semantics=("arbitrary", "arbitrary"), vmem_limit_bytes=VMEM_LIMIT),
        name="gla",
    )(q, k, v, la, gate, gain)


def _outproj_kernel(x_ref, osb_ref, ogla_ref, mod_ref, g_ref, wsb_ref, wgla_ref, x1_ref, h2_ref):
    mixed = _dot(osb_ref[0], wsb_ref[...]) + _dot(ogla_ref[0], wgla_ref[...])
    gate1 = mod_ref[0, 2:3, :]
    shift2 = mod_ref[0, 3:4, :]
    scale2 = mod_ref[0, 4:5, :]
    x1 = x_ref[0] + (1.0 + gate1) * mixed
    x1_ref[0] = x1
    h2_ref[0] = (_rms_normalise(x1, g_ref[...]) * (1.0 + scale2) + shift2).astype(BF16)


def _outproj(x, o_sb, o_gla, mod, g2, w_sb, w_gla):
    b, s, d = x.shape
    tm = ROW_TILE
    const = lambda bi, i: (0, 0)
    row = lambda bi, i: (bi, i, 0)
    return pl.pallas_call(
        _outproj_kernel,
        out_shape=[jax.ShapeDtypeStruct((b, s, d), F32), jax.ShapeDtypeStruct((b, s, d), BF16)],
        grid=(b, s // tm),
        in_specs=[
            pl.BlockSpec((1, tm, d), row),
            pl.BlockSpec((1, tm, SB_WIDTH), row),
            pl.BlockSpec((1, tm, GLA_VWIDTH), row),
            pl.BlockSpec((1, N_MOD, d), lambda bi, i: (bi, 0, 0)),
            pl.BlockSpec((1, d), const),
            pl.BlockSpec(w_sb.shape, const, pipeline_mode=pl.Buffered(1)),
            pl.BlockSpec(w_gla.shape, const, pipeline_mode=pl.Buffered(1)),
        ],
        out_specs=[pl.BlockSpec((1, tm, d), row), pl.BlockSpec((1, tm, d), row)],
        compiler_params=pltpu.CompilerParams(
            dimension_semantics=("arbitrary", "arbitrary"), vmem_limit_bytes=VMEM_LIMIT),
        name="outproj",
    )(x, o_sb, o_gla, mod, g2, w_sb, w_gla)


def _ffn_kernel(h_ref, x1_ref, mod_ref, gfin_ref, wup_ref, wconv_ref, bconv_ref, wdown_ref,
                o_ref, tail_ref, stage_ref):
    tm = h_ref.shape[1]
    d_ff = wdown_ref.shape[0]
    fc = FFN_CHUNK

    @pl.when(pl.program_id(1) == 0)
    def _():
        tail_ref[...] = jnp.zeros_like(tail_ref)

    h = h_ref[0]

    n_chunks = d_ff // fc

    def up(ci):
        cols = (slice(ci * fc, (ci + 1) * fc), slice(d_ff + ci * fc, d_ff + (ci + 1) * fc))
        return tuple((_dot(h, wup_ref[:, c]), c) for c in cols)

    def conv(u, cols, slot):
        tail = tail_ref[:, cols]
        slabs = fc // LANES
        for t in range(slabs):
            lanes = slice(t * LANES, (t + 1) * LANES)
            stage_ref[slot * slabs + t, 0:SUBLANES, :] = tail[:, lanes]
            stage_ref[slot * slabs + t, SUBLANES:, :] = u[:, lanes]
        tail_ref[:, cols] = u[tm - SUBLANES:, :]
        out = bconv_ref[:, cols] + wconv_ref[CONV_WIDTH - 1:CONV_WIDTH, cols] * u
        for back in range(1, CONV_WIDTH):
            shifted = jnp.concatenate(
                [stage_ref[slot * slabs + t, pl.ds(SUBLANES - back, tm, stride=1), :]
                 for t in range(slabs)], axis=1)
            out = out + wconv_ref[CONV_WIDTH - 1 - back:CONV_WIDTH - back, cols] * shifted
        return out

    acc = jnp.zeros((tm, o_ref.shape[2]), F32)
    pending = up(0)
    for ci in range(n_chunks):
        (uv, vcols), (ug, gcols) = pending
        if ci + 1 < n_chunks:
            pending = up(ci + 1)
        val = conv(uv, vcols, 0)
        gte = conv(ug, gcols, 1)
        act = val * (gte * (1.0 / (1.0 + jnp.exp(-gte))))
        acc = acc + _dot(act.astype(BF16), wdown_ref[ci * fc:(ci + 1) * fc, :])

    gate2 = mod_ref[0, 5:6, :]
    y = x1_ref[0] + (1.0 + gate2) * acc
    o_ref[0] = _rms_normalise(y, gfin_ref[...])


def _ffn(h2, x1, mod, g_final, w_up, w_conv, b_conv, w_down):
    b, s, d = x1.shape
    tm = ROW_TILE
    const = lambda bi, i: (0, 0)
    row = lambda bi, i: (bi, i, 0)
    return pl.pallas_call(
        _ffn_kernel,
        out_shape=jax.ShapeDtypeStruct((b, s, d), F32),
        grid=(b, s // tm),
        in_specs=[
            pl.BlockSpec((1, tm, d), row),
            pl.BlockSpec((1, tm, d), row),
            pl.BlockSpec((1, N_MOD, d), lambda bi, i: (bi, 0, 0)),
            pl.BlockSpec((1, d), const),
            pl.BlockSpec(w_up.shape, const, pipeline_mode=pl.Buffered(1)),
            pl.BlockSpec(w_conv.shape, const),
            pl.BlockSpec(b_conv.shape, const),
            pl.BlockSpec(w_down.shape, const, pipeline_mode=pl.Buffered(1)),
        ],
        out_specs=pl.BlockSpec((1, tm, d), row),
        scratch_shapes=[pltpu.VMEM((SUBLANES, w_up.shape[1]), F32),
                        pltpu.VMEM((2 * FFN_CHUNK // LANES, SUBLANES + tm, LANES), F32)],
        compiler_params=pltpu.CompilerParams(
            dimension_semantics=("arbitrary", "arbitrary"), vmem_limit_bytes=VMEM_LIMIT),
        name="ffn",
    )(h2, x1, mod, g_final, w_up, w_conv, b_conv, w_down)


def kernel(x, c, w_ada, b_ada, g_norm1, w_in, w_fg2, b_fg2, g_gla_out, w_out,
           g_norm2, w_up, w_conv, b_conv, w_down, g_final):
    depth = w_ada.shape[0]
    bsz, _, d = x.shape
    sb_cols = 3 * SB_WIDTH
    gla_cols = 2 * GLA_KWIDTH + 2 * GLA_VWIDTH
    out = x
    for l in range(depth):
        c_pad = jnp.zeros((SUBLANES, d), F32).at[:bsz].set(c)
        mod = _adaln(c_pad, w_ada[l], b_ada[l][None, :])[:bsz].reshape(bsz, N_MOD, d)

        w_sb = w_in[l][:, :sb_cols].astype(BF16)
        w_gla = jnp.pad(w_in[l][:, sb_cols:], ((0, 0), (0, LANES - GLA_GATE_RANK))).astype(BF16)
        w_fg = jnp.pad(w_fg2[l], ((0, LANES - GLA_GATE_RANK), (0, 0)))
        assert w_gla.shape[1] == gla_cols + LANES
        sbq, sbk, sbv, gq, gk, gv, gg, la = _inproj(
            out, mod, g_norm1[l][None, :], w_sb, w_gla, w_fg, b_fg2[l][None, :])

        o_sb = _sb_attention(sbq, sbk, sbv)
        o_gla = _gla(gq, gk, gv, la, gg, g_gla_out[l][None, :])

        w_o = w_out[l].astype(BF16)
        x1, h2 = _outproj(out, o_sb, o_gla, mod, g_norm2[l][None, :],
                          w_o[:SB_WIDTH], w_o[SB_WIDTH:])
        assert depth == 1
        out = _ffn(h2, x1, mod, g_final[None, :], w_up[l].astype(BF16), w_conv[l],
                   b_conv[l][None, :], w_down[l].astype(BF16))
    return out
```

```python
import functools

import jax
import jax.numpy as jnp
from jax import lax
from jax.experimental import pallas as pl
from jax.experimental.pallas import tpu as pltpu

F32 = jnp.float32
BF16 = jnp.bfloat16

EPS = 1e-6
SB_HEAD_DIM = 64
SB_WIDTH = 512
GLA_HEADS = 4
GLA_DK = 64
GLA_DV = 128
GLA_KWIDTH = GLA_HEADS * GLA_DK
GLA_VWIDTH = GLA_HEADS * GLA_DV
GLA_GATE_RANK = 16
GLA_TAU = 16.0
CHUNK = 64
CONV_WIDTH = 3
N_MOD = 6

LANES = 128
SUBLANES = 8
VMEM_LIMIT = 56 * 1024 * 1024

ROW_TILE = 512
INPROJ_TILE = 1024
SUB_TILE = 256
SB_BLOCK = 128
SB_LOG_ZERO = -110.0
SB_WINDOW = 3
FFN_CHUNK = 256


def _dot(a, b):
    return jnp.dot(a, b, preferred_element_type=F32)


def _dot_nt(a, b):
    return lax.dot_general(a, b, (((1,), (1,)), ((), ())), preferred_element_type=F32)


def _dot_tn(a, b):
    return lax.dot_general(a, b, (((0,), (0,)), ((), ())), preferred_element_type=F32)


def _split_bf16(a):
    hi = a.astype(BF16)
    lo = (a - hi.astype(F32)).astype(BF16)
    return hi, lo


def _rms_normalise(x, gain):
    ms = jnp.mean(x * x, axis=-1, keepdims=True)
    return x * lax.rsqrt(ms + EPS) * gain


def _adaln_kernel(c_ref, w_ref, b_ref, o_ref):
    c = c_ref[...]
    a = c * (1.0 / (1.0 + jnp.exp(-c)))
    a_hi, a_lo = _split_bf16(a)
    w_hi, w_lo = _split_bf16(w_ref[...])
    o_ref[...] = (_dot(a_hi, w_hi) + _dot(a_lo, w_hi) + _dot(a_hi, w_lo)) + b_ref[...]


def _adaln(c_pad, w_ada, b_ada):
    rows, d = c_pad.shape
    n = w_ada.shape[1]
    tn = 1024
    return pl.pallas_call(
        _adaln_kernel,
        out_shape=jax.ShapeDtypeStruct((rows, n), F32),
        grid=(n // tn,),
        in_specs=[
            pl.BlockSpec((rows, d), lambda j: (0, 0)),
            pl.BlockSpec((d, tn), lambda j: (0, j)),
            pl.BlockSpec((1, tn), lambda j: (0, j)),
        ],
        out_specs=pl.BlockSpec((rows, tn), lambda j: (0, j)),
        compiler_params=pltpu.CompilerParams(
            dimension_semantics=("arbitrary",), vmem_limit_bytes=VMEM_LIMIT),
        name="adaln",
    )(c_pad, w_ada, b_ada)


def _inproj_kernel(x_ref, mod_ref, g_ref, wsb_ref, wgla_ref, wfg_ref, bfg_ref,
                   sbq_ref, sbk_ref, sbv_ref, gq_ref, gk_ref, gv_ref, gg_ref, la_ref):
    tm = x_ref.shape[1]
    sub = SUB_TILE
    shift = mod_ref[0, 0:1, :]
    scale = mod_ref[0, 1:2, :]
    gain = g_ref[...]
    wfg_hi, wfg_lo = _split_bf16(wfg_ref[...])
    w = SB_WIDTH
    kw, vw = GLA_KWIDTH, GLA_VWIDTH
    o = 2 * kw

    def normalise(si):
        x = x_ref[0, si * sub:(si + 1) * sub, :]
        return (_rms_normalise(x, gain) * (1.0 + scale) + shift).astype(BF16)

    def project(hb, si):
        rows = slice(si * sub, (si + 1) * sub)
        sbq_ref[0, rows, :] = (_dot(hb, wsb_ref[:, 0:w]) * (SB_HEAD_DIM ** -0.5)).astype(BF16)
        sbk_ref[0, rows, :] = _dot(hb, wsb_ref[:, w:2 * w]).astype(BF16)
        sbv_ref[0, rows, :] = _dot(hb, wsb_ref[:, 2 * w:3 * w]).astype(BF16)
        gq_ref[0, rows, :] = (_dot(hb, wgla_ref[:, 0:kw]) * (GLA_DK ** -0.5)).astype(BF16)
        gk_ref[0, rows, :] = _dot(hb, wgla_ref[:, kw:2 * kw]).astype(BF16)
        gv_ref[0, rows, :] = _dot(hb, wgla_ref[:, o:o + vw]).astype(BF16)
        gg_ref[0, rows, :] = _dot(hb, wgla_ref[:, o + vw:o + 2 * vw]).astype(BF16)
        gf = _dot(hb, wgla_ref[:, o + 2 * vw:o + 2 * vw + LANES])
        gf_hi, gf_lo = _split_bf16(gf)
        pre = (_dot(gf_hi, wfg_hi) + _dot(gf_lo, wfg_hi) + _dot(gf_hi, wfg_lo)) + bfg_ref[...]
        log_sig = jnp.minimum(pre, 0.0) - jnp.log(1.0 + jnp.exp(-jnp.abs(pre)))
        la_ref[0, rows, :] = log_sig * (1.0 / GLA_TAU)

    pending = normalise(0)
    for si in range(tm // sub):
        hb = pending
        if si + 1 < tm // sub:
            pending = normalise(si + 1)
        project(hb, si)


def _inproj(x, mod, g1, w_sb, w_gla, w_fg, b_fg):
    b, s, d = x.shape
    tm = INPROJ_TILE
    const = lambda bi, i: (0, 0)
    row = lambda bi, i: (bi, i, 0)
    widths = (SB_WIDTH, SB_WIDTH, SB_WIDTH, GLA_KWIDTH, GLA_KWIDTH, GLA_VWIDTH, GLA_VWIDTH)
    out_shape = [jax.ShapeDtypeStruct((b, s, wd), BF16) for wd in widths]
    out_shape.append(jax.ShapeDtypeStruct((b, s, GLA_KWIDTH), F32))
    out_specs = [pl.BlockSpec((1, tm, wd), row) for wd in widths]
    out_specs.append(pl.BlockSpec((1, tm, GLA_KWIDTH), row))
    return pl.pallas_call(
        _inproj_kernel,
        out_shape=out_shape,
        grid=(b, s // tm),
        in_specs=[
            pl.BlockSpec((1, tm, d), row),
            pl.BlockSpec((1, N_MOD, d), lambda bi, i: (bi, 0, 0)),
            pl.BlockSpec((1, d), const),
            pl.BlockSpec(w_sb.shape, const, pipeline_mode=pl.Buffered(1)),
            pl.BlockSpec(w_gla.shape, const, pipeline_mode=pl.Buffered(1)),
            pl.BlockSpec(w_fg.shape, const),
            pl.BlockSpec(b_fg.shape, const),
        ],
        out_specs=out_specs,
        compiler_params=pltpu.CompilerParams(
            dimension_semantics=("arbitrary", "arbitrary"), vmem_limit_bytes=VMEM_LIMIT),
        name="inproj",
    )(x, mod, g1, w_sb, w_gla, w_fg, b_fg)


def _sb_kernel(q_ref, k_ref, v_ref, o_ref, acc_ref, run_ref):
    blk = SB_BLOCK
    pairs = q_ref.shape[2] // LANES
    i = pl.program_id(1)
    lane = lax.broadcasted_iota(jnp.int32, (1, LANES), 1)
    first = lane < SB_HEAD_DIM
    q_heads = []
    for p in range(pairs):
        q = q_ref[0, :, p * LANES:(p + 1) * LANES]
        zq = jnp.zeros_like(q)
        q_heads += [jnp.where(first, q, zq), jnp.where(first, zq, q)]

    kj = lax.broadcasted_iota(jnp.int32, (blk, 2 * blk), 0)
    ks = lax.broadcasted_iota(jnp.int32, (blk, 2 * blk), 1)
    tri = jnp.where((kj > ks) | (ks >= blk), 1.0, 0.0).astype(BF16)

    t_pos = lax.broadcasted_iota(jnp.int32, (blk, blk), 0)
    s_pos = lax.broadcasted_iota(jnp.int32, (blk, blk), 1)
    causal = s_pos < t_pos

    heads = range(2 * pairs)

    def key_blocks(kbs, fresh):
        n = len(kbs)
        starts = [pl.multiple_of(kb * blk, blk) for kb in kbs]
        scores = [[_dot_nt(q_heads[h],
                           k_ref[0, pl.ds(starts[j], blk), (h // 2) * LANES:(h // 2 + 1) * LANES])
                   for h in heads] for j in range(n)]
        log_keep = [[None] * len(heads) for _ in range(n)]
        log_rest = [[None] * len(heads) for _ in range(n)]
        for j in range(n):
            for h in heads:
                z = scores[j][h]
                rest = jnp.minimum(-z, 0.0) - jnp.log(1.0 + jnp.exp(-jnp.abs(z)))
                if fresh and j == 0:
                    rest = jnp.where(causal, rest, 0.0)
                log_rest[j][h] = rest
                log_keep[j][h] = z + rest
        sums = [[None] * len(heads) for _ in range(n)]
        for j in range(n):
            for h in heads:
                hi, lo = _split_bf16(log_rest[j][h])
                sums[j][h] = _dot(hi, tri) + _dot(lo, tri)
        weights = [[None] * len(heads) for _ in range(n)]
        for h in heads:
            run = None if fresh else run_ref[h]
            for j in range(n):
                arg = log_keep[j][h] + sums[j][h][:, :blk]
                wgt = jnp.exp(arg if run is None else arg + run)
                if fresh and j == 0:
                    wgt = jnp.where(causal, wgt, 0.0)
                total = sums[j][h][:, blk:]
                run = total if run is None else run + total
                weights[j][h] = wgt.astype(BF16)
            run_ref[h] = run
        for p in range(pairs):
            cols = slice(p * LANES, (p + 1) * LANES)
            w_parts, v_parts = [], []
            for j in range(n):
                v = v_ref[0, pl.ds(starts[j], blk), cols]
                zv = jnp.zeros_like(v)
                w_parts += weights[j][2 * p:2 * p + 2]
                v_parts += [jnp.where(first, v, zv), jnp.where(first, zv, v)]
            out = _dot(jnp.concatenate(w_parts, axis=1), jnp.concatenate(v_parts, axis=0))
            acc_ref[:, cols] = out if fresh else acc_ref[:, cols] + out

    window = SB_WINDOW

    @pl.when(i >= window - 1)
    def _():
        key_blocks([i - j for j in range(window)], True)

    @pl.when(i < window - 1)
    def _():
        key_blocks([i], True)

    def cond(state):
        kb, top = state
        return jnp.logical_and(kb >= 0, top > SB_LOG_ZERO)

    def run_top():
        top = run_ref[0]
        for h in heads[1:]:
            top = jnp.maximum(top, run_ref[h])
        return jnp.max(top)

    def body(state):
        kb, _ = state
        key_blocks([kb], False)
        return kb - 1, run_top()

    lax.while_loop(cond, body, (jnp.where(i >= window - 1, i - window, i - 1), run_top()))
    o_ref[0] = acc_ref[...].astype(BF16)


def _sb_attention(q, k, v):
    b, s, w = q.shape
    blk = SB_BLOCK
    heads = w // SB_HEAD_DIM
    kv_spec = pl.BlockSpec((1, s, w), lambda bi, i: (bi, 0, 0), pipeline_mode=pl.Buffered(1))
    q_spec = pl.BlockSpec((1, blk, w), lambda bi, i: (bi, i, 0))
    return pl.pallas_call(
        _sb_kernel,
        out_shape=jax.ShapeDtypeStruct((b, s, w), BF16),
        grid=(b, s // blk),
        in_specs=[q_spec, kv_spec, kv_spec],
        out_specs=q_spec,
        scratch_shapes=[pltpu.VMEM((blk, w), F32), pltpu.VMEM((heads, blk, blk), F32)],
        compiler_params=pltpu.CompilerParams(
            dimension_semantics=("arbitrary", "arbitrary"),
            vmem_limit_bytes=VMEM_LIMIT),
        name="sb_attn",
    )(q, k, v)


def _gla_kernel(q_ref, k_ref, v_ref, la_ref, gate_ref, gain_ref, later_ref, member_ref,
                o_ref, state_ref, o_scr):
    tm = q_ref.shape[1]
    n_chunks = tm // CHUNK

    @pl.when(pl.program_id(1) == 0)
    def _():
        state_ref[...] = jnp.zeros_like(state_ref)

    la_hi, la_lo = _split_bf16(la_ref[0])

    later = later_ref[...]
    rev = _dot(later, la_hi) + _dot(later, la_lo)
    k_dec = (k_ref[0].astype(F32) * jnp.exp(rev)).astype(BF16)

    member = member_ref[...]
    decay = jnp.exp(_dot(member, la_hi) + _dot(member, la_lo))

    pairs = GLA_HEADS // 2
    pr = lax.broadcasted_iota(jnp.int32, (2 * GLA_DV, LANES), 0)
    pc = lax.broadcasted_iota(jnp.int32, (2 * GLA_DV, LANES), 1)
    own_head = (pr >= GLA_DV) == (pc >= GLA_DK)

    q = q_ref[0]
    v = v_ref[0]
    zero_block = jnp.zeros((2 * GLA_DV, LANES), BF16)
    for ci in range(n_chunks):
        rows = slice(ci * CHUNK, (ci + 1) * CHUNK)
        kv = _dot_tn(v[rows], k_dec[rows])
        block_rows = []
        for p in range(pairs):
            lanes = slice(p * LANES, (p + 1) * LANES)
            kv_pair = kv[p * 2 * GLA_DV:(p + 1) * 2 * GLA_DV, lanes]
            state = decay[ci:ci + 1, lanes] * state_ref[p] + jnp.where(own_head, kv_pair, 0.0)
            state_ref[p] = state
            parts = [zero_block] * pairs
            parts[p] = state.astype(BF16)
            block_rows.append(jnp.concatenate(parts, axis=1))
        o_scr[rows, :] = _dot_nt(q[rows], jnp.concatenate(block_rows, axis=0))

    gain = gain_ref[...]
    gate = gate_ref[0].astype(F32)
    swish = gate * (1.0 / (1.0 + jnp.exp(-gate)))
    for h in range(GLA_HEADS):
        cols = slice(h * GLA_DV, (h + 1) * GLA_DV)
        oh = o_scr[:, cols]
        ms = jnp.mean(oh * oh, axis=-1, keepdims=True)
        o_ref[0, :, cols] = (oh * lax.rsqrt(ms + EPS) * gain[:, cols] * swish[:, cols]).astype(BF16)


def _gla(q, k, v, la, gate, gain):
    b, s, _ = q.shape
    tm = ROW_TILE
    row = lambda bi, i: (bi, i, 0)
    const = lambda bi, i: (0, 0)
    n_chunks = tm // CHUNK
    t_idx = jnp.arange(tm)
    later = ((t_idx[:, None] // CHUNK == t_idx[None, :] // CHUNK)
             & (t_idx[None, :] > t_idx[:, None])).astype(BF16)
    member = (jnp.arange(2 * SUBLANES)[:, None] == t_idx[None, :] // CHUNK).astype(BF16)
    assert n_chunks <= 2 * SUBLANES
    return pl.pallas_call(
        _gla_kernel,
        out_shape=jax.ShapeDtypeStruct((b, s, GLA_VWIDTH), BF16),
        grid=(b, s // tm),
        in_specs=[
            pl.BlockSpec((1, tm, GLA_KWIDTH), row),
            pl.BlockSpec((1, tm, GLA_KWIDTH), row),
            pl.BlockSpec((1, tm, GLA_VWIDTH), row),
            pl.BlockSpec((1, tm, GLA_KWIDTH), row),
            pl.BlockSpec((1, tm, GLA_VWIDTH), row),
            pl.BlockSpec((1, GLA_VWIDTH), const),
            pl.BlockSpec(later.shape, const),
            pl.BlockSpec(member.shape, const),
        ],
        out_specs=pl.BlockSpec((1, tm, GLA_VWIDTH), row),
        scratch_shapes=[pltpu.VMEM((GLA_HEADS // 2, 2 * GLA_DV, 2 * GLA_DK), F32),
                        pltpu.VMEM((tm, GLA_VWIDTH), F32)],
        compiler_params=pltpu.CompilerParams(
            dimension_semantics=("arbitrary", "arbitrary"), vmem_limit_bytes=VMEM_LIMIT),
        name="gla",
    )(q, k, v, la, gate, gain, later, member)


def _outproj_kernel(x_ref, osb_ref, ogla_ref, mod_ref, g_ref, wsb_ref, wgla_ref, x1_ref, h2_ref):
    mixed = _dot(osb_ref[0], wsb_ref[...]) + _dot(ogla_ref[0], wgla_ref[...])
    gate1 = mod_ref[0, 2:3, :]
    shift2 = mod_ref[0, 3:4, :]
    scale2 = mod_ref[0, 4:5, :]
    x1 = x_ref[0] + (1.0 + gate1) * mixed
    x1_ref[0] = x1
    h2_ref[0] = (_rms_normalise(x1, g_ref[...]) * (1.0 + scale2) + shift2).astype(BF16)


def _outproj(x, o_sb, o_gla, mod, g2, w_sb, w_gla):
    b, s, d = x.shape
    tm = ROW_TILE
    const = lambda bi, i: (0, 0)
    row = lambda bi, i: (bi, i, 0)
    return pl.pallas_call(
        _outproj_kernel,
        out_shape=[jax.ShapeDtypeStruct((b, s, d), F32), jax.ShapeDtypeStruct((b, s, d), BF16)],
        grid=(b, s // tm),
        in_specs=[
            pl.BlockSpec((1, tm, d), row),
            pl.BlockSpec((1, tm, SB_WIDTH), row),
            pl.BlockSpec((1, tm, GLA_VWIDTH), row),
            pl.BlockSpec((1, N_MOD, d), lambda bi, i: (bi, 0, 0)),
            pl.BlockSpec((1, d), const),
            pl.BlockSpec(w_sb.shape, const, pipeline_mode=pl.Buffered(1)),
            pl.BlockSpec(w_gla.shape, const, pipeline_mode=pl.Buffered(1)),
        ],
        out_specs=[pl.BlockSpec((1, tm, d), row), pl.BlockSpec((1, tm, d), row)],
        compiler_params=pltpu.CompilerParams(
            dimension_semantics=("arbitrary", "arbitrary"), vmem_limit_bytes=VMEM_LIMIT),
        name="outproj",
    )(x, o_sb, o_gla, mod, g2, w_sb, w_gla)


def _ffn_kernel(h_ref, x1_ref, mod_ref, gfin_ref, wup_ref, wconv_ref, bconv_ref, wdown_ref,
                o_ref, tail_ref, stage_ref):
    tm = h_ref.shape[1]
    d_ff = wdown_ref.shape[0]
    fc = FFN_CHUNK

    @pl.when(pl.program_id(1) == 0)
    def _():
        tail_ref[...] = jnp.zeros_like(tail_ref)

    h = h_ref[0]

    n_chunks = d_ff // fc

    def up(ci):
        cols = (slice(ci * fc, (ci + 1) * fc), slice(d_ff + ci * fc, d_ff + (ci + 1) * fc))
        return tuple((_dot(h, wup_ref[:, c]), c) for c in cols)

    def conv(u, cols, slot):
        tail = tail_ref[:, cols]
        slabs = fc // LANES
        for t in range(slabs):
            lanes = slice(t * LANES, (t + 1) * LANES)
            stage_ref[slot * slabs + t, 0:SUBLANES, :] = tail[:, lanes]
            stage_ref[slot * slabs + t, SUBLANES:, :] = u[:, lanes]
        tail_ref[:, cols] = u[tm - SUBLANES:, :]
        out = bconv_ref[:, cols] + wconv_ref[CONV_WIDTH - 1:CONV_WIDTH, cols] * u
        for back in range(1, CONV_WIDTH):
            shifted = jnp.concatenate(
                [stage_ref[slot * slabs + t, pl.ds(SUBLANES - back, tm, stride=1), :]
                 for t in range(slabs)], axis=1)
            out = out + wconv_ref[CONV_WIDTH - 1 - back:CONV_WIDTH - back, cols] * shifted
        return out

    acc = jnp.zeros((tm, o_ref.shape[2]), F32)
    pending = up(0)
    for ci in range(n_chunks):
        (uv, vcols), (ug, gcols) = pending
        if ci + 1 < n_chunks:
            pending = up(ci + 1)
        val = conv(uv, vcols, 0)
        gte = conv(ug, gcols, 1)
        act = val * (gte * (1.0 / (1.0 + jnp.exp(-gte))))
        acc = acc + _dot(act.astype(BF16), wdown_ref[ci * fc:(ci + 1) * fc, :])

    gate2 = mod_ref[0, 5:6, :]
    y = x1_ref[0] + (1.0 + gate2) * acc
    o_ref[0] = _rms_normalise(y, gfin_ref[...])


def _ffn(h2, x1, mod, g_final, w_up, w_conv, b_conv, w_down):
    b, s, d = x1.shape
    tm = ROW_TILE
    const = lambda bi, i: (0, 0)
    row = lambda bi, i: (bi, i, 0)
    return pl.pallas_call(
        _ffn_kernel,
        out_shape=jax.ShapeDtypeStruct((b, s, d), F32),
        grid=(b, s // tm),
        in_specs=[
            pl.BlockSpec((1, tm, d), row),
            pl.BlockSpec((1, tm, d), row),
            pl.BlockSpec((1, N_MOD, d), lambda bi, i: (bi, 0, 0)),
            pl.BlockSpec((1, d), const),
            pl.BlockSpec(w_up.shape, const, pipeline_mode=pl.Buffered(1)),
            pl.BlockSpec(w_conv.shape, const),
            pl.BlockSpec(b_conv.shape, const),
            pl.BlockSpec(w_down.shape, const, pipeline_mode=pl.Buffered(1)),
        ],
        out_specs=pl.BlockSpec((1, tm, d), row),
        scratch_shapes=[pltpu.VMEM((SUBLANES, w_up.shape[1]), F32),
                        pltpu.VMEM((2 * FFN_CHUNK // LANES, SUBLANES + tm, LANES), F32)],
        compiler_params=pltpu.CompilerParams(
            dimension_semantics=("arbitrary", "arbitrary"), vmem_limit_bytes=VMEM_LIMIT),
        name="ffn",
    )(h2, x1, mod, g_final, w_up, w_conv, b_conv, w_down)


def kernel(x, c, w_ada, b_ada, g_norm1, w_in, w_fg2, b_fg2, g_gla_out, w_out,
           g_norm2, w_up, w_conv, b_conv, w_down, g_final):
    depth = w_ada.shape[0]
    bsz, _, d = x.shape
    sb_cols = 3 * SB_WIDTH
    gla_cols = 2 * GLA_KWIDTH + 2 * GLA_VWIDTH
    out = x
    for l in range(depth):
        c_pad = jnp.zeros((SUBLANES, d), F32).at[:bsz].set(c)
        mod = _adaln(c_pad, w_ada[l], b_ada[l][None, :])[:bsz].reshape(bsz, N_MOD, d)

        w_sb = w_in[l][:, :sb_cols].astype(BF16)
        w_gla = jnp.pad(w_in[l][:, sb_cols:], ((0, 0), (0, LANES - GLA_GATE_RANK))).astype(BF16)
        w_fg = jnp.pad(w_fg2[l], ((0, LANES - GLA_GATE_RANK), (0, 0)))
        assert w_gla.shape[1] == gla_cols + LANES
        sbq, sbk, sbv, gq, gk, gv, gg, la = _inproj(
            out, mod, g_norm1[l][None, :], w_sb, w_gla, w_fg, b_fg2[l][None, :])

        o_sb = _sb_attention(sbq, sbk, sbv)
        o_gla = _gla(gq, gk, gv, la, gg, g_gla_out[l][None, :])

        w_o = w_out[l].astype(BF16)
        x1, h2 = _outproj(out, o_sb, o_gla, mod, g_norm2[l][None, :],
                          w_o[:SB_WIDTH], w_o[SB_WIDTH:])
        assert depth == 1
        out = _ffn(h2, x1, mod, g_final[None, :], w_up[l].astype(BF16), w_conv[l],
                   b_conv[l][None, :], w_down[l].astype(BF16))
    return out
```

```python
import functools

import jax
import jax.numpy as jnp
from jax import lax
from jax.experimental import pallas as pl
from jax.experimental.pallas import tpu as pltpu

F32 = jnp.float32
BF16 = jnp.bfloat16

EPS = 1e-6
SB_HEAD_DIM = 64
SB_WIDTH = 512
GLA_HEADS = 4
GLA_DK = 64
GLA_DV = 128
GLA_KWIDTH = GLA_HEADS * GLA_DK
GLA_VWIDTH = GLA_HEADS * GLA_DV
GLA_GATE_RANK = 16
GLA_TAU = 16.0
CHUNK = 64
CONV_WIDTH = 3
N_MOD = 6

LANES = 128
SUBLANES = 8
VMEM_LIMIT = 56 * 1024 * 1024

ROW_TILE = 512
INPROJ_TILE = 1024
SUB_TILE = 256
SB_BLOCK = 128
SB_DROP_ZERO = 160.0
LOG2_E = 1.4426950408889634
SB_WINDOW = 3
SB_QUERY_TILE = 512
FFN_CHUNK = 256


def _dot(a, b):
    return jnp.dot(a, b, preferred_element_type=F32)


def _dot_nt(a, b):
    return lax.dot_general(a, b, (((1,), (1,)), ((), ())), preferred_element_type=F32)


def _dot_tn(a, b):
    return lax.dot_general(a, b, (((0,), (0,)), ((), ())), preferred_element_type=F32)


def _split_bf16(a):
    hi = a.astype(BF16)
    lo = (a - hi.astype(F32)).astype(BF16)
    return hi, lo


def _rms_normalise(x, gain):
    ms = jnp.mean(x * x, axis=-1, keepdims=True)
    return x * lax.rsqrt(ms + EPS) * gain


def _adaln_kernel(c_ref, w_ref, b_ref, o_ref):
    c = c_ref[...]
    a = c * (1.0 / (1.0 + jnp.exp(-c)))
    a_hi, a_lo = _split_bf16(a)
    w_hi, w_lo = _split_bf16(w_ref[...])
    o_ref[...] = (_dot(a_hi, w_hi) + _dot(a_lo, w_hi) + _dot(a_hi, w_lo)) + b_ref[...]


def _adaln(c_pad, w_ada, b_ada):
    rows, d = c_pad.shape
    n = w_ada.shape[1]
    tn = 1024
    return pl.pallas_call(
        _adaln_kernel,
        out_shape=jax.ShapeDtypeStruct((rows, n), F32),
        grid=(n // tn,),
        in_specs=[
            pl.BlockSpec((rows, d), lambda j: (0, 0)),
            pl.BlockSpec((d, tn), lambda j: (0, j)),
            pl.BlockSpec((1, tn), lambda j: (0, j)),
        ],
        out_specs=pl.BlockSpec((rows, tn), lambda j: (0, j)),
        compiler_params=pltpu.CompilerParams(
            dimension_semantics=("arbitrary",), vmem_limit_bytes=VMEM_LIMIT),
        name="adaln",
    )(c_pad, w_ada, b_ada)


def _inproj_kernel(x_ref, mod_ref, g_ref, win_ref, wgf_ref, wfg_ref, bfg_ref,
                   wout_ref, wup_ref, wdown_ref,
                   sbq_ref, sbk_ref, sbv_ref, gq_ref, gk_ref, gv_ref, gg_ref, la_ref,
                   wout_bf_ref, wup_bf_ref, wdown_bf_ref):
    wout_bf_ref[...] = wout_ref[...].astype(BF16)
    wup_bf_ref[...] = wup_ref[...].astype(BF16)
    wdown_bf_ref[...] = wdown_ref[...].astype(BF16)

    tm = x_ref.shape[1]
    sub = SUB_TILE
    shift = mod_ref[0, 0:1, :]
    scale = mod_ref[0, 1:2, :]
    gain = g_ref[...]
    wfg_hi, wfg_lo = _split_bf16(wfg_ref[...])
    w = SB_WIDTH
    kw, vw = GLA_KWIDTH, GLA_VWIDTH
    o = 3 * w

    def normalise(si):
        x = x_ref[0, si * sub:(si + 1) * sub, :]
        return (_rms_normalise(x, gain) * (1.0 + scale) + shift).astype(BF16)

    def project(hb, si):
        rows = slice(si * sub, (si + 1) * sub)
        sbq_ref[0, rows, :] = (_dot(hb, win_ref[:, 0:w])
                               * (LOG2_E * SB_HEAD_DIM ** -0.5)).astype(BF16)
        sbk_ref[0, rows, :] = _dot(hb, win_ref[:, w:2 * w]).astype(BF16)
        sbv_ref[0, rows, :] = _dot(hb, win_ref[:, 2 * w:3 * w]).astype(BF16)
        gq_ref[0, rows, :] = (_dot(hb, win_ref[:, o:o + kw]) * (GLA_DK ** -0.5)).astype(BF16)
        gk_ref[0, rows, :] = _dot(hb, win_ref[:, o + kw:o + 2 * kw]).astype(BF16)
        ov = o + 2 * kw
        gv_ref[0, rows, :] = _dot(hb, win_ref[:, ov:ov + vw]).astype(BF16)
        gg_ref[0, rows, :] = _dot(hb, win_ref[:, ov + vw:ov + 2 * vw]).astype(BF16)
        gf = _dot(hb, wgf_ref[...])
        gf_hi, gf_lo = _split_bf16(gf)
        pre = (_dot(gf_hi, wfg_hi) + _dot(gf_lo, wfg_hi) + _dot(gf_hi, wfg_lo)) + bfg_ref[...]
        log_sig = jnp.minimum(pre, 0.0) - jnp.log(1.0 + jnp.exp(-jnp.abs(pre)))
        la_ref[0, rows, :] = log_sig * (1.0 / GLA_TAU)

    pending = normalise(0)
    for si in range(tm // sub):
        hb = pending
        if si + 1 < tm // sub:
            pending = normalise(si + 1)
        project(hb, si)


def _inproj(x, mod, g1, w_in, w_gf, w_fg, b_fg, later_weights):
    b, s, d = x.shape
    tm = INPROJ_TILE
    steps = b * (s // tm)
    const = lambda bi, i: (0, 0)
    row = lambda bi, i: (bi, i, 0)
    slab = lambda bi, i: (bi * (s // tm) + i, 0)
    widths = (SB_WIDTH, SB_WIDTH, SB_WIDTH, GLA_KWIDTH, GLA_KWIDTH, GLA_VWIDTH, GLA_VWIDTH)
    out_shape = [jax.ShapeDtypeStruct((b, s, wd), BF16) for wd in widths]
    out_shape.append(jax.ShapeDtypeStruct((b, s, GLA_KWIDTH), F32))
    out_specs = [pl.BlockSpec((1, tm, wd), row) for wd in widths]
    out_specs.append(pl.BlockSpec((1, tm, GLA_KWIDTH), row))
    weight_specs = []
    for wgt in later_weights:
        rows, cols = wgt.shape
        slab_rows = rows // steps
        assert slab_rows * steps == rows and slab_rows % (2 * SUBLANES) == 0
        weight_specs.append(pl.BlockSpec((slab_rows, cols), slab))
        out_shape.append(jax.ShapeDtypeStruct(wgt.shape, BF16))
    out_specs += weight_specs
    return pl.pallas_call(
        _inproj_kernel,
        out_shape=out_shape,
        grid=(b, s // tm),
        in_specs=[
            pl.BlockSpec((1, tm, d), row),
            pl.BlockSpec((1, N_MOD, d), lambda bi, i: (bi, 0, 0)),
            pl.BlockSpec((1, d), const),
            pl.BlockSpec(w_in.shape, const, pipeline_mode=pl.Buffered(1)),
            pl.BlockSpec(w_gf.shape, const),
            pl.BlockSpec(w_fg.shape, const),
            pl.BlockSpec(b_fg.shape, const),
        ] + weight_specs,
        out_specs=out_specs,
        compiler_params=pltpu.CompilerParams(
            dimension_semantics=("arbitrary", "arbitrary"), vmem_limit_bytes=VMEM_LIMIT),
        name="inproj",
    )(x, mod, g1, w_in, w_gf, w_fg, b_fg, *later_weights)


def _sb_kernel(q_ref, k_ref, v_ref, o_ref, acc_ref, run_ref):
    blk = SB_BLOCK
    pairs = q_ref.shape[2] // LANES
    lane = lax.broadcasted_iota(jnp.int32, (1, LANES), 1)
    first = lane < SB_HEAD_DIM

    kj = lax.broadcasted_iota(jnp.int32, (2 * blk, 2 * blk), 0) & (blk - 1)
    ks = lax.broadcasted_iota(jnp.int32, (2 * blk, 2 * blk), 1)
    tri = jnp.where((kj > ks) | (ks >= blk), 1.0, 0.0).astype(BF16)

    t_pos = lax.broadcasted_iota(jnp.int32, (blk, blk), 0)
    s_pos = lax.broadcasted_iota(jnp.int32, (blk, blk), 1)
    causal = s_pos < t_pos

    heads = range(2 * pairs)

    def key_blocks(q_pairs, kbs, fresh):
        n = len(kbs)
        starts = [pl.multiple_of(kb * blk, blk) for kb in kbs]
        scores = [[None] * len(heads) for _ in range(n)]
        for j in range(n):
            for p in range(pairs):
                k = k_ref[0, pl.ds(starts[j], blk), p * LANES:(p + 1) * LANES]
                zk = jnp.zeros_like(k)
                k_heads = jnp.concatenate([jnp.where(first, k, zk), jnp.where(first, zk, k)], axis=0)
                both = _dot_nt(q_pairs[p], k_heads)
                scores[j][2 * p] = both[:, :blk]
                scores[j][2 * p + 1] = both[:, blk:]
        log_keep = [[None] * len(heads) for _ in range(n)]
        drop = [[None] * len(heads) for _ in range(n)]
        for j in range(n):
            for h in heads:
                z = scores[j][h]
                d = jnp.maximum(z, 0.0) + jnp.log2(1.0 + jnp.exp2(-jnp.abs(z)))
                if fresh and j == 0:
                    d = jnp.where(causal, d, 0.0)
                drop[j][h] = d
                log_keep[j][h] = z - d
        sums = [[None] * len(heads) for _ in range(n)]
        for j in range(n):
            for h in heads:
                hi, lo = _split_bf16(drop[j][h])
                sums[j][h] = _dot(jnp.concatenate([hi, lo], axis=1), tri)
        weights = [[None] * len(heads) for _ in range(n)]
        for h in heads:
            run = None if fresh else run_ref[h]
            for j in range(n):
                arg = log_keep[j][h] - sums[j][h][:, :blk]
                wgt = jnp.exp2(arg if run is None else arg - run)
                if fresh and j == 0:
                    wgt = jnp.where(causal, wgt, 0.0)
                total = sums[j][h][:, blk:]
                run = total if run is None else run + total
                weights[j][h] = wgt.astype(BF16)
            run_ref[h] = run
        for p in range(pairs):
            cols = slice(p * LANES, (p + 1) * LANES)
            w_parts, v_parts = [], []
            for j in range(n):
                v = v_ref[0, pl.ds(starts[j], blk), cols]
                zv = jnp.zeros_like(v)
                w_parts += weights[j][2 * p:2 * p + 2]
                v_parts += [jnp.where(first, v, zv), jnp.where(first, zv, v)]
            out = _dot(jnp.concatenate(w_parts, axis=1), jnp.concatenate(v_parts, axis=0))
            acc_ref[:, cols] = out if fresh else acc_ref[:, cols] + out

    window = SB_WINDOW
    n_sub = q_ref.shape[1] // blk

    def least_drop():
        least = run_ref[0]
        for h in heads[1:]:
            least = jnp.minimum(least, run_ref[h])
        return jnp.min(least)

    def query_block(sub, carry):
        i = pl.program_id(1) * n_sub + sub
        rows = pl.ds(pl.multiple_of(sub * blk, blk), blk)
        q_pairs = [q_ref[0, rows, p * LANES:(p + 1) * LANES] for p in range(pairs)]

        @pl.when(i >= window - 1)
        def _():
            key_blocks(q_pairs, [i - j for j in range(window)], True)

        @pl.when(i < window - 1)
        def _():
            key_blocks(q_pairs, [i], True)

        def cond(state):
            kb, least = state
            return jnp.logical_and(kb >= 0, least < SB_DROP_ZERO)

        def body(state):
            kb, _ = state
            key_blocks(q_pairs, [kb], False)
            return kb - 1, least_drop()

        lax.while_loop(cond, body, (jnp.where(i >= window - 1, i - window, i - 1), least_drop()))
        o_ref[0, rows, :] = acc_ref[...].astype(BF16)
        return carry

    lax.fori_loop(0, n_sub, query_block, 0)


def _sb_attention(q, k, v):
    b, s, w = q.shape
    blk = SB_BLOCK
    heads = w // SB_HEAD_DIM
    kv_spec = pl.BlockSpec((1, s, w), lambda bi, i: (bi, 0, 0), pipeline_mode=pl.Buffered(1))
    tq = SB_QUERY_TILE
    q_spec = pl.BlockSpec((1, tq, w), lambda bi, i: (bi, i, 0))
    return pl.pallas_call(
        _sb_kernel,
        out_shape=jax.ShapeDtypeStruct((b, s, w), BF16),
        grid=(b, s // tq),
        in_specs=[q_spec, kv_spec, kv_spec],
        out_specs=q_spec,
        scratch_shapes=[pltpu.VMEM((blk, w), F32), pltpu.VMEM((heads, blk, blk), F32)],
        compiler_params=pltpu.CompilerParams(
            dimension_semantics=("arbitrary", "arbitrary"),
            vmem_limit_bytes=VMEM_LIMIT),
        name="sb_attn",
    )(q, k, v)


def _gla_kernel(q_ref, k_ref, v_ref, la_ref, gate_ref, gain_ref, later_ref, member_ref,
                o_ref, state_ref, o_scr):
    tm = q_ref.shape[1]
    n_chunks = tm // CHUNK

    @pl.when(pl.program_id(1) == 0)
    def _():
        state_ref[...] = jnp.zeros_like(state_ref)

    la_hi, la_lo = _split_bf16(la_ref[0])

    later = later_ref[...]
    rev = _dot(later, la_hi) + _dot(later, la_lo)
    k_dec = (k_ref[0].astype(F32) * jnp.exp(rev)).astype(BF16)

    member = member_ref[...]
    decay = jnp.exp(_dot(member, la_hi) + _dot(member, la_lo))

    pairs = GLA_HEADS // 2
    pr = lax.broadcasted_iota(jnp.int32, (2 * GLA_DV, LANES), 0)
    pc = lax.broadcasted_iota(jnp.int32, (2 * GLA_DV, LANES), 1)
    own_head = (pr >= GLA_DV) == (pc >= GLA_DK)

    q = q_ref[0]
    v = v_ref[0]
    zero_block = jnp.zeros((2 * GLA_DV, LANES), BF16)
    for ci in range(n_chunks):
        rows = slice(ci * CHUNK, (ci + 1) * CHUNK)
        kv = _dot_tn(v[rows], k_dec[rows])
        block_rows = []
        for p in range(pairs):
            lanes = slice(p * LANES, (p + 1) * LANES)
            kv_pair = kv[p * 2 * GLA_DV:(p + 1) * 2 * GLA_DV, lanes]
            state = decay[ci:ci + 1, lanes] * state_ref[p] + jnp.where(own_head, kv_pair, 0.0)
            state_ref[p] = state
            parts = [zero_block] * pairs
            parts[p] = state.astype(BF16)
            block_rows.append(jnp.concatenate(parts, axis=1))
        o_scr[rows, :] = _dot_nt(q[rows], jnp.concatenate(block_rows, axis=0))

    gain = gain_ref[...]
    gate = gate_ref[0].astype(F32)
    swish = gate * (1.0 / (1.0 + jnp.exp(-gate)))
    for h in range(GLA_HEADS):
        cols = slice(h * GLA_DV, (h + 1) * GLA_DV)
        oh = o_scr[:, cols]
        ms = jnp.mean(oh * oh, axis=-1, keepdims=True)
        o_ref[0, :, cols] = (oh * lax.rsqrt(ms + EPS) * gain[:, cols] * swish[:, cols]).astype(BF16)


def _gla(q, k, v, la, gate, gain):
    b, s, _ = q.shape
    tm = ROW_TILE
    row = lambda bi, i: (bi, i, 0)
    const = lambda bi, i: (0, 0)
    n_chunks = tm // CHUNK
    t_idx = jnp.arange(tm)
    later = ((t_idx[:, None] // CHUNK == t_idx[None, :] // CHUNK)
             & (t_idx[None, :] > t_idx[:, None])).astype(BF16)
    member = (jnp.arange(2 * SUBLANES)[:, None] == t_idx[None, :] // CHUNK).astype(BF16)
    assert n_chunks <= 2 * SUBLANES
    return pl.pallas_call(
        _gla_kernel,
        out_shape=jax.ShapeDtypeStruct((b, s, GLA_VWIDTH), BF16),
        grid=(b, s // tm),
        in_specs=[
            pl.BlockSpec((1, tm, GLA_KWIDTH), row),
            pl.BlockSpec((1, tm, GLA_KWIDTH), row),
            pl.BlockSpec((1, tm, GLA_VWIDTH), row),
            pl.BlockSpec((1, tm, GLA_KWIDTH), row),
            pl.BlockSpec((1, tm, GLA_VWIDTH), row),
            pl.BlockSpec((1, GLA_VWIDTH), const),
            pl.BlockSpec(later.shape, const),
            pl.BlockSpec(member.shape, const),
        ],
        out_specs=pl.BlockSpec((1, tm, GLA_VWIDTH), row),
        scratch_shapes=[pltpu.VMEM((GLA_HEADS // 2, 2 * GLA_DV, 2 * GLA_DK), F32),
                        pltpu.VMEM((tm, GLA_VWIDTH), F32)],
        compiler_params=pltpu.CompilerParams(
            dimension_semantics=("arbitrary", "arbitrary"), vmem_limit_bytes=VMEM_LIMIT),
        name="gla",
    )(q, k, v, la, gate, gain, later, member)


def _ffn_kernel(x_ref, osb_ref, ogla_ref, mod_ref, g2_ref, gfin_ref, wout_ref, wup_ref,
                wconv_ref, bconv_ref, wdown_ref, o_ref, tail_ref, stage_ref, x1_ref):
    tm = x_ref.shape[1]
    d_ff = wdown_ref.shape[0]
    fc = FFN_CHUNK
    n_chunks = d_ff // fc

    @pl.when(pl.program_id(1) == 0)
    def _():
        tail_ref[...] = jnp.zeros_like(tail_ref)

    gate1 = mod_ref[0, 2:3, :]
    shift2 = mod_ref[0, 3:4, :]
    scale2 = mod_ref[0, 4:5, :]
    gain2 = g2_ref[...]

    def chunk_cols(ci):
        return (slice(ci * fc, (ci + 1) * fc), slice(d_ff + ci * fc, d_ff + (ci + 1) * fc))

    def mix(rows):
        mixed = (_dot(osb_ref[0, rows, :], wout_ref[0:SB_WIDTH, :])
                 + _dot(ogla_ref[0, rows, :], wout_ref[SB_WIDTH:SB_WIDTH + GLA_VWIDTH, :]))
        x1 = x_ref[0, rows, :] + (1.0 + gate1) * mixed
        x1_ref[rows, :] = x1
        return (_rms_normalise(x1, gain2) * (1.0 + scale2) + shift2).astype(BF16)

    halves = (slice(0, tm // 2), slice(tm // 2, tm))
    h_parts, first_up = [], []
    for rows in halves:
        hp = mix(rows)
        h_parts.append(hp)
        first_up.append([_dot(hp, wup_ref[:, c]) for c in chunk_cols(0)])
    h = jnp.concatenate(h_parts, axis=0)

    def up(ci):
        return tuple((_dot(h, wup_ref[:, c]), c) for c in chunk_cols(ci))

    def conv(u, cols, slot):
        tail = tail_ref[:, cols]
        slabs = fc // LANES
        for t in range(slabs):
            lanes = slice(t * LANES, (t + 1) * LANES)
            stage_ref[slot * slabs + t, 0:SUBLANES, :] = tail[:, lanes]
            stage_ref[slot * slabs + t, SUBLANES:, :] = u[:, lanes]
        tail_ref[:, cols] = u[tm - SUBLANES:, :]
        out = bconv_ref[:, cols] + wconv_ref[CONV_WIDTH - 1:CONV_WIDTH, cols] * u
        for back in range(1, CONV_WIDTH):
            shifted = jnp.concatenate(
                [stage_ref[slot * slabs + t, pl.ds(SUBLANES - back, tm, stride=1), :]
                 for t in range(slabs)], axis=1)
            out = out + wconv_ref[CONV_WIDTH - 1 - back:CONV_WIDTH - back, cols] * shifted
        return out

    acc = jnp.zeros((tm, o_ref.shape[2]), F32)
    pending = tuple((jnp.concatenate([fu[k] for fu in first_up], axis=0), c)
                    for k, c in enumerate(chunk_cols(0)))
    for ci in range(n_chunks):
        (uv, vcols), (ug, gcols) = pending
        if ci + 1 < n_chunks:
            pending = up(ci + 1)
        val = conv(uv, vcols, 0)
        gte = conv(ug, gcols, 1)
        act = val * (gte * (1.0 / (1.0 + jnp.exp(-gte))))
        acc = acc + _dot(act.astype(BF16), wdown_ref[ci * fc:(ci + 1) * fc, :])

    gate2 = mod_ref[0, 5:6, :]
    y = x1_ref[...] + (1.0 + gate2) * acc
    o_ref[0] = _rms_normalise(y, gfin_ref[...])


def _mix_ffn(x, o_sb, o_gla, mod, g2, g_final, w_o, w_up, w_conv, b_conv, w_down):
    b, s, d = x.shape
    tm = ROW_TILE
    const = lambda bi, i: (0, 0)
    row = lambda bi, i: (bi, i, 0)
    return pl.pallas_call(
        _ffn_kernel,
        out_shape=jax.ShapeDtypeStruct((b, s, d), F32),
        grid=(b, s // tm),
        in_specs=[
            pl.BlockSpec((1, tm, d), row),
            pl.BlockSpec((1, tm, SB_WIDTH), row),
            pl.BlockSpec((1, tm, GLA_VWIDTH), row),
            pl.BlockSpec((1, N_MOD, d), lambda bi, i: (bi, 0, 0)),
            pl.BlockSpec((1, d), const),
            pl.BlockSpec((1, d), const),
            pl.BlockSpec(w_o.shape, const, pipeline_mode=pl.Buffered(1)),
            pl.BlockSpec(w_up.shape, const, pipeline_mode=pl.Buffered(1)),
            pl.BlockSpec(w_conv.shape, const),
            pl.BlockSpec(b_conv.shape, const),
            pl.BlockSpec(w_down.shape, const, pipeline_mode=pl.Buffered(1)),
        ],
        out_specs=pl.BlockSpec((1, tm, d), row),
        scratch_shapes=[pltpu.VMEM((SUBLANES, w_up.shape[1]), F32),
                        pltpu.VMEM((2 * FFN_CHUNK // LANES, SUBLANES + tm, LANES), F32),
                        pltpu.VMEM((tm, d), F32)],
        compiler_params=pltpu.CompilerParams(
            dimension_semantics=("arbitrary", "arbitrary"), vmem_limit_bytes=VMEM_LIMIT),
        name="mix_ffn",
    )(x, o_sb, o_gla, mod, g2, g_final, w_o, w_up, w_conv, b_conv, w_down)


def kernel(x, c, w_ada, b_ada, g_norm1, w_in, w_fg2, b_fg2, g_gla_out, w_out,
           g_norm2, w_up, w_conv, b_conv, w_down, g_final):
    depth = w_ada.shape[0]
    bsz, _, d = x.shape
    gf_col = 3 * SB_WIDTH + 2 * GLA_KWIDTH + 2 * GLA_VWIDTH
    out = x
    for l in range(depth):
        c_pad = jnp.zeros((SUBLANES, d), F32).at[:bsz].set(c)
        mod = _adaln(c_pad, w_ada[l], b_ada[l][None, :])[:bsz].reshape(bsz, N_MOD, d)

        assert w_in.shape[2] == gf_col + GLA_GATE_RANK
        w_gf = jnp.pad(w_in[l][:, gf_col:], ((0, 0), (0, LANES - GLA_GATE_RANK))).astype(BF16)
        w_fg = jnp.pad(w_fg2[l], ((0, LANES - GLA_GATE_RANK), (0, 0)))
        sbq, sbk, sbv, gq, gk, gv, gg, la, w_o, w_u, w_d = _inproj(
            out, mod, g_norm1[l][None, :], w_in[l].astype(BF16), w_gf, w_fg, b_fg2[l][None, :],
            (w_out[l], w_up[l], w_down[l]))

        o_sb = _sb_attention(sbq, sbk, sbv)
        o_gla = _gla(gq, gk, gv, la, gg, g_gla_out[l][None, :])

        assert depth == 1
        out = _mix_ffn(out, o_sb, o_gla, mod, g_norm2[l][None, :], g_final[None, :],
                       w_o, w_u, w_conv[l], b_conv[l][None, :], w_d)
    return out
```

```python
import functools

import jax
import jax.numpy as jnp
from jax import lax
from jax.experimental import pallas as pl
from jax.experimental.pallas import tpu as pltpu

F32 = jnp.float32
BF16 = jnp.bfloat16

EPS = 1e-6
SB_HEAD_DIM = 64
SB_WIDTH = 512
GLA_HEADS = 4
GLA_DK = 64
GLA_DV = 128
GLA_KWIDTH = GLA_HEADS * GLA_DK
GLA_VWIDTH = GLA_HEADS * GLA_DV
GLA_GATE_RANK = 16
GLA_TAU = 16.0
CHUNK = 64
CONV_WIDTH = 3
N_MOD = 6

LANES = 128
SUBLANES = 8
VMEM_LIMIT = 56 * 1024 * 1024

ROW_TILE = 512
INPROJ_TILE = 1024
SUB_TILE = 256
SB_BLOCK = 128
SB_DROP_ZERO = 150.0
LOG2_E = 1.4426950408889634
SB_WINDOW = 3
SB_QUERY_TILE = 512
FFN_CHUNK = 256


def _dot(a, b):
    return jnp.dot(a, b, preferred_element_type=F32)


def _dot_nt(a, b):
    return lax.dot_general(a, b, (((1,), (1,)), ((), ())), preferred_element_type=F32)


def _dot_tn(a, b):
    return lax.dot_general(a, b, (((0,), (0,)), ((), ())), preferred_element_type=F32)


def _split_bf16(a):
    hi = a.astype(BF16)
    lo = (a - hi.astype(F32)).astype(BF16)
    return hi, lo


def _rms_normalise(x, gain):
    ms = jnp.mean(x * x, axis=-1, keepdims=True)
    return x * lax.rsqrt(ms + EPS) * gain


def _adaln_kernel(c_ref, w_ref, b_ref, o_ref):
    c = c_ref[...]
    a = c * (1.0 / (1.0 + jnp.exp(-c)))
    a_hi, a_lo = _split_bf16(a)
    w_hi, w_lo = _split_bf16(w_ref[...])
    o_ref[...] = (_dot(a_hi, w_hi) + _dot(a_lo, w_hi) + _dot(a_hi, w_lo)) + b_ref[...]


def _adaln(c_pad, w_ada, b_ada):
    rows, d = c_pad.shape
    n = w_ada.shape[1]
    tn = 1024
    return pl.pallas_call(
        _adaln_kernel,
        out_shape=jax.ShapeDtypeStruct((rows, n), F32),
        grid=(n // tn,),
        in_specs=[
            pl.BlockSpec((rows, d), lambda j: (0, 0)),
            pl.BlockSpec((d, tn), lambda j: (0, j)),
            pl.BlockSpec((1, tn), lambda j: (0, j)),
        ],
        out_specs=pl.BlockSpec((rows, tn), lambda j: (0, j)),
        compiler_params=pltpu.CompilerParams(
            dimension_semantics=("arbitrary",), vmem_limit_bytes=VMEM_LIMIT),
        name="adaln",
    )(c_pad, w_ada, b_ada)


def _inproj_kernel(x_ref, mod_ref, g_ref, win_ref, wgvt_ref, wgf_ref, wfg_ref, bfg_ref,
                   wout_ref, wup_ref, wdown_ref,
                   sbq_ref, sbk_ref, sbv_ref, gq_ref, gk_ref, gvt_ref, gg_ref, la_ref,
                   wout_bf_ref, wup_bf_ref, wdown_bf_ref):
    wout_bf_ref[...] = wout_ref[...].astype(BF16)
    wup_bf_ref[...] = wup_ref[...].astype(BF16)
    wdown_bf_ref[...] = wdown_ref[...].astype(BF16)

    tm = x_ref.shape[1]
    sub = SUB_TILE
    shift = mod_ref[0, 0:1, :]
    scale = mod_ref[0, 1:2, :]
    gain = g_ref[...]
    wfg_hi, wfg_lo = _split_bf16(wfg_ref[...])
    w = SB_WIDTH
    kw, vw = GLA_KWIDTH, GLA_VWIDTH
    o = 3 * w

    def normalise(si):
        x = x_ref[0, si * sub:(si + 1) * sub, :]
        return (_rms_normalise(x, gain) * (1.0 + scale) + shift).astype(BF16)

    def project(hb, si):
        rows = slice(si * sub, (si + 1) * sub)
        sbq_ref[0, rows, :] = (_dot(hb, win_ref[:, 0:w])
                               * (LOG2_E * SB_HEAD_DIM ** -0.5)).astype(BF16)
        sbk_ref[0, rows, :] = _dot(hb, win_ref[:, w:2 * w]).astype(BF16)
        sbv_ref[0, rows, :] = _dot(hb, win_ref[:, 2 * w:3 * w]).astype(BF16)
        gq_ref[0, rows, :] = (_dot(hb, win_ref[:, o:o + kw]) * (GLA_DK ** -0.5)).astype(BF16)
        gk_ref[0, rows, :] = _dot(hb, win_ref[:, o + kw:o + 2 * kw]).astype(BF16)
        ov = o + 2 * kw
        gvt_ref[0, :, rows] = _dot_nt(wgvt_ref[...], hb).astype(BF16)
        gg_ref[0, rows, :] = _dot(hb, win_ref[:, ov + vw:ov + 2 * vw]).astype(BF16)
        gf = _dot(hb, wgf_ref[...])
        gf_hi, gf_lo = _split_bf16(gf)
        pre = (_dot(gf_hi, wfg_hi) + _dot(gf_lo, wfg_hi) + _dot(gf_hi, wfg_lo)) + bfg_ref[...]
        log_sig = jnp.minimum(pre, 0.0) - jnp.log(1.0 + jnp.exp(-jnp.abs(pre)))
        la_ref[0, rows, :] = log_sig * (1.0 / GLA_TAU)

    pending = normalise(0)
    for si in range(tm // sub):
        hb = pending
        if si + 1 < tm // sub:
            pending = normalise(si + 1)
        project(hb, si)


def _inproj(x, mod, g1, w_in, w_gvt, w_gf, w_fg, b_fg, later_weights):
    b, s, d = x.shape
    tm = INPROJ_TILE
    steps = b * (s // tm)
    const = lambda bi, i: (0, 0)
    row = lambda bi, i: (bi, i, 0)
    slab = lambda bi, i: (bi * (s // tm) + i, 0)
    widths = (SB_WIDTH, SB_WIDTH, SB_WIDTH, GLA_KWIDTH, GLA_KWIDTH, None, GLA_VWIDTH)
    out_shape, out_specs = [], []
    for wd in widths:
        if wd is None:
            out_shape.append(jax.ShapeDtypeStruct((b, GLA_VWIDTH, s), BF16))
            out_specs.append(pl.BlockSpec((1, GLA_VWIDTH, tm), lambda bi, i: (bi, 0, i)))
        else:
            out_shape.append(jax.ShapeDtypeStruct((b, s, wd), BF16))
            out_specs.append(pl.BlockSpec((1, tm, wd), row))
    out_shape.append(jax.ShapeDtypeStruct((b, s, GLA_KWIDTH), F32))
    out_specs.append(pl.BlockSpec((1, tm, GLA_KWIDTH), row))
    weight_specs = []
    for wgt in later_weights:
        rows, cols = wgt.shape
        slab_rows = rows // steps
        assert slab_rows * steps == rows and slab_rows % (2 * SUBLANES) == 0
        weight_specs.append(pl.BlockSpec((slab_rows, cols), slab))
        out_shape.append(jax.ShapeDtypeStruct(wgt.shape, BF16))
    out_specs += weight_specs
    return pl.pallas_call(
        _inproj_kernel,
        out_shape=out_shape,
        grid=(b, s // tm),
        in_specs=[
            pl.BlockSpec((1, tm, d), row),
            pl.BlockSpec((1, N_MOD, d), lambda bi, i: (bi, 0, 0)),
            pl.BlockSpec((1, d), const),
            pl.BlockSpec(w_in.shape, const, pipeline_mode=pl.Buffered(1)),
            pl.BlockSpec(w_gvt.shape, const),
            pl.BlockSpec(w_gf.shape, const),
            pl.BlockSpec(w_fg.shape, const),
            pl.BlockSpec(b_fg.shape, const),
        ] + weight_specs,
        out_specs=out_specs,
        compiler_params=pltpu.CompilerParams(
            dimension_semantics=("arbitrary", "arbitrary"), vmem_limit_bytes=VMEM_LIMIT),
        name="inproj",
    )(x, mod, g1, w_in, w_gvt, w_gf, w_fg, b_fg, *later_weights)


def _sb_kernel(q_ref, k_ref, v_ref, o_ref, acc_ref, run_ref):
    blk = SB_BLOCK
    pairs = q_ref.shape[2] // LANES
    lane = lax.broadcasted_iota(jnp.int32, (1, LANES), 1)
    first = lane < SB_HEAD_DIM

    kj = lax.broadcasted_iota(jnp.int32, (2 * blk, 2 * blk), 0) & (blk - 1)
    ks = lax.broadcasted_iota(jnp.int32, (2 * blk, 2 * blk), 1)
    tri = jnp.where((kj > ks) | (ks >= blk), 1.0, 0.0).astype(BF16)

    t_pos = lax.broadcasted_iota(jnp.int32, (blk, blk), 0)
    s_pos = lax.broadcasted_iota(jnp.int32, (blk, blk), 1)
    causal = s_pos < t_pos

    heads = range(2 * pairs)

    def key_blocks(q_pairs, kbs, fresh):
        n = len(kbs)
        starts = [pl.multiple_of(kb * blk, blk) for kb in kbs]
        scores = [[None] * len(heads) for _ in range(n)]
        for j in range(n):
            for p in range(pairs):
                k = k_ref[0, pl.ds(starts[j], blk), p * LANES:(p + 1) * LANES]
                zk = jnp.zeros_like(k)
                k_heads = jnp.concatenate([jnp.where(first, k, zk), jnp.where(first, zk, k)], axis=0)
                both = _dot_nt(q_pairs[p], k_heads)
                scores[j][2 * p] = both[:, :blk]
                scores[j][2 * p + 1] = both[:, blk:]
        log_keep = [[None] * len(heads) for _ in range(n)]
        drop = [[None] * len(heads) for _ in range(n)]
        for j in range(n):
            for h in heads:
                z = scores[j][h]
                d = jnp.maximum(z, 0.0) + jnp.log2(1.0 + jnp.exp2(-jnp.abs(z)))
                if fresh and j == 0:
                    d = jnp.where(causal, d, 0.0)
                drop[j][h] = d
                log_keep[j][h] = z - d
        sums = [[None] * len(heads) for _ in range(n)]
        for j in range(n):
            for h in heads:
                hi, lo = _split_bf16(drop[j][h])
                sums[j][h] = _dot(jnp.concatenate([hi, lo], axis=1), tri)
        weights = [[None] * len(heads) for _ in range(n)]
        for h in heads:
            run = None if fresh else run_ref[h]
            for j in range(n):
                arg = log_keep[j][h] - sums[j][h][:, :blk]
                wgt = jnp.exp2(arg if run is None else arg - run)
                if fresh and j == 0:
                    wgt = jnp.where(causal, wgt, 0.0)
                total = sums[j][h][:, blk:]
                run = total if run is None else run + total
                weights[j][h] = wgt.astype(BF16)
            run_ref[h] = run
        for p in range(pairs):
            cols = slice(p * LANES, (p + 1) * LANES)
            w_parts, v_parts = [], []
            for j in range(n):
                v = v_ref[0, pl.ds(starts[j], blk), cols]
                zv = jnp.zeros_like(v)
                w_parts += weights[j][2 * p:2 * p + 2]
                v_parts += [jnp.where(first, v, zv), jnp.where(first, zv, v)]
            out = _dot(jnp.concatenate(w_parts, axis=1), jnp.concatenate(v_parts, axis=0))
            acc_ref[:, cols] = out if fresh else acc_ref[:, cols] + out

    window = SB_WINDOW
    n_sub = q_ref.shape[1] // blk

    def least_drop():
        least = run_ref[0]
        for h in heads[1:]:
            least = jnp.minimum(least, run_ref[h])
        return jnp.min(least)

    def query_block(sub, carry):
        i = pl.program_id(1) * n_sub + sub
        rows = pl.ds(pl.multiple_of(sub * blk, blk), blk)
        q_pairs = [q_ref[0, rows, p * LANES:(p + 1) * LANES] for p in range(pairs)]

        @pl.when(i >= window - 1)
        def _():
            key_blocks(q_pairs, [i - j for j in range(window)], True)

        @pl.when(i < window - 1)
        def _():
            key_blocks(q_pairs, [i], True)

        def cond(state):
            kb, least = state
            return jnp.logical_and(kb >= 0, least < SB_DROP_ZERO)

        def body(state):
            kb, _ = state
            key_blocks(q_pairs, [kb], False)
            return kb - 1, least_drop()

        lax.while_loop(cond, body, (jnp.where(i >= window - 1, i - window, i - 1), least_drop()))
        o_ref[0, rows, :] = acc_ref[...].astype(BF16)
        return carry

    lax.fori_loop(0, n_sub, query_block, 0)


def _sb_attention(q, k, v):
    b, s, w = q.shape
    blk = SB_BLOCK
    heads = w // SB_HEAD_DIM
    kv_spec = pl.BlockSpec((1, s, w), lambda bi, i: (bi, 0, 0), pipeline_mode=pl.Buffered(1))
    tq = SB_QUERY_TILE
    q_spec = pl.BlockSpec((1, tq, w), lambda bi, i: (bi, i, 0))
    return pl.pallas_call(
        _sb_kernel,
        out_shape=jax.ShapeDtypeStruct((b, s, w), BF16),
        grid=(b, s // tq),
        in_specs=[q_spec, kv_spec, kv_spec],
        out_specs=q_spec,
        scratch_shapes=[pltpu.VMEM((blk, w), F32), pltpu.VMEM((heads, blk, blk), F32)],
        compiler_params=pltpu.CompilerParams(
            dimension_semantics=("arbitrary", "arbitrary"),
            vmem_limit_bytes=VMEM_LIMIT),
        name="sb_attn",
    )(q, k, v)


def _gla_kernel(q_ref, k_ref, vt_ref, la_ref, gate_ref, gain_ref, later_ref, member_ref,
                o_ref, state_ref, o_scr):
    tm = q_ref.shape[1]
    n_chunks = tm // CHUNK

    @pl.when(pl.program_id(1) == 0)
    def _():
        state_ref[...] = jnp.zeros_like(state_ref)

    la_hi, la_lo = _split_bf16(la_ref[0])

    later = later_ref[...]
    rev = _dot(later, la_hi) + _dot(later, la_lo)
    k_dec = (k_ref[0].astype(F32) * jnp.exp(rev)).astype(BF16)

    member = member_ref[...]
    decay = jnp.exp(_dot(member, la_hi) + _dot(member, la_lo))

    pairs = GLA_HEADS // 2
    pr = lax.broadcasted_iota(jnp.int32, (2 * GLA_DV, LANES), 0)
    pc = lax.broadcasted_iota(jnp.int32, (2 * GLA_DV, LANES), 1)
    own_head = (pr >= GLA_DV) == (pc >= GLA_DK)

    q = q_ref[0]
    zero_block = jnp.zeros((2 * GLA_DV, LANES), BF16)
    per_tile = LANES // CHUNK
    frame = lax.broadcasted_iota(jnp.int32, (LANES, 1), 0)
    updates = []
    for ci in range(n_chunks):
        tile = slice((ci // per_tile) * LANES, (ci // per_tile + 1) * LANES)
        in_chunk = (frame >= (ci % per_tile) * CHUNK) & (frame < (ci % per_tile + 1) * CHUNK)
        k_tile = k_dec[tile]
        k_tile = jnp.where(in_chunk, k_tile, jnp.zeros_like(k_tile))
        updates.append([
            jnp.where(own_head,
                      _dot(vt_ref[0, p * 2 * GLA_DV:(p + 1) * 2 * GLA_DV, tile],
                           k_tile[:, p * LANES:(p + 1) * LANES]), 0.0)
            for p in range(pairs)])

    states = [state_ref[p] for p in range(pairs)]
    readouts = []
    for ci in range(n_chunks):
        block_rows = []
        for p in range(pairs):
            lanes = slice(p * LANES, (p + 1) * LANES)
            states[p] = decay[ci:ci + 1, lanes] * states[p] + updates[ci][p]
            parts = [zero_block] * pairs
            parts[p] = states[p].astype(BF16)
            block_rows.append(jnp.concatenate(parts, axis=1))
        readouts.append(jnp.concatenate(block_rows, axis=0))
    for p in range(pairs):
        state_ref[p] = states[p]
    for ci in range(n_chunks):
        rows = slice(ci * CHUNK, (ci + 1) * CHUNK)
        o_scr[rows, :] = _dot_nt(q[rows], readouts[ci])

    gain = gain_ref[...]
    gate = gate_ref[0].astype(F32)
    swish = gate * (1.0 / (1.0 + jnp.exp(-gate)))
    for h in range(GLA_HEADS):
        cols = slice(h * GLA_DV, (h + 1) * GLA_DV)
        oh = o_scr[:, cols]
        ms = jnp.mean(oh * oh, axis=-1, keepdims=True)
        o_ref[0, :, cols] = (oh * lax.rsqrt(ms + EPS) * gain[:, cols] * swish[:, cols]).astype(BF16)


def _gla(q, k, v_t, la, gate, gain):
    b, s, _ = q.shape
    tm = ROW_TILE
    row = lambda bi, i: (bi, i, 0)
    const = lambda bi, i: (0, 0)
    n_chunks = tm // CHUNK
    t_idx = jnp.arange(tm)
    later = ((t_idx[:, None] // CHUNK == t_idx[None, :] // CHUNK)
             & (t_idx[None, :] > t_idx[:, None])).astype(BF16)
    member = (jnp.arange(2 * SUBLANES)[:, None] == t_idx[None, :] // CHUNK).astype(BF16)
    assert n_chunks <= 2 * SUBLANES
    return pl.pallas_call(
        _gla_kernel,
        out_shape=jax.ShapeDtypeStruct((b, s, GLA_VWIDTH), BF16),
        grid=(b, s // tm),
        in_specs=[
            pl.BlockSpec((1, tm, GLA_KWIDTH), row),
            pl.BlockSpec((1, tm, GLA_KWIDTH), row),
            pl.BlockSpec((1, GLA_VWIDTH, tm), lambda bi, i: (bi, 0, i)),
            pl.BlockSpec((1, tm, GLA_KWIDTH), row),
            pl.BlockSpec((1, tm, GLA_VWIDTH), row),
            pl.BlockSpec((1, GLA_VWIDTH), const),
            pl.BlockSpec(later.shape, const),
            pl.BlockSpec(member.shape, const),
        ],
        out_specs=pl.BlockSpec((1, tm, GLA_VWIDTH), row),
        scratch_shapes=[pltpu.VMEM((GLA_HEADS // 2, 2 * GLA_DV, 2 * GLA_DK), F32),
                        pltpu.VMEM((tm, GLA_VWIDTH), F32)],
        compiler_params=pltpu.CompilerParams(
            dimension_semantics=("arbitrary", "arbitrary"), vmem_limit_bytes=VMEM_LIMIT),
        name="gla",
    )(q, k, v_t, la, gate, gain, later, member)


def _ffn_kernel(x_ref, osb_ref, ogla_ref, mod_ref, g2_ref, gfin_ref, wout_ref, wup_ref,
                wconv_ref, bconv_ref, wdown_ref, o_ref, tail_ref, stage_ref, x1_ref):
    tm = x_ref.shape[1]
    d_ff = wdown_ref.shape[0]
    fc = FFN_CHUNK
    n_chunks = d_ff // fc

    @pl.when(pl.program_id(1) == 0)
    def _():
        tail_ref[...] = jnp.zeros_like(tail_ref)

    gate1 = mod_ref[0, 2:3, :]
    shift2 = mod_ref[0, 3:4, :]
    scale2 = mod_ref[0, 4:5, :]
    gain2 = g2_ref[...]

    def chunk_cols(ci):
        return (slice(ci * fc, (ci + 1) * fc), slice(d_ff + ci * fc, d_ff + (ci + 1) * fc))

    def mix(rows):
        mixed = (_dot(osb_ref[0, rows, :], wout_ref[0:SB_WIDTH, :])
                 + _dot(ogla_ref[0, rows, :], wout_ref[SB_WIDTH:SB_WIDTH + GLA_VWIDTH, :]))
        x1 = x_ref[0, rows, :] + (1.0 + gate1) * mixed
        x1_ref[rows, :] = x1
        return (_rms_normalise(x1, gain2) * (1.0 + scale2) + shift2).astype(BF16)

    halves = (slice(0, tm // 2), slice(tm // 2, tm))
    h_parts, first_up = [], []
    for rows in halves:
        hp = mix(rows)
        h_parts.append(hp)
        first_up.append([_dot(hp, wup_ref[:, c]) for c in chunk_cols(0)])
    h = jnp.concatenate(h_parts, axis=0)

    def up(ci):
        return tuple((_dot(h, wup_ref[:, c]), c) for c in chunk_cols(ci))

    def conv(u, cols, slot):
        tail = tail_ref[:, cols]
        slabs = fc // LANES
        for t in range(slabs):
            lanes = slice(t * LANES, (t + 1) * LANES)
            stage_ref[slot * slabs + t, 0:SUBLANES, :] = tail[:, lanes]
            stage_ref[slot * slabs + t, SUBLANES:, :] = u[:, lanes]
        tail_ref[:, cols] = u[tm - SUBLANES:, :]
        out = bconv_ref[:, cols] + wconv_ref[CONV_WIDTH - 1:CONV_WIDTH, cols] * u
        for back in range(1, CONV_WIDTH):
            shifted = jnp.concatenate(
                [stage_ref[slot * slabs + t, pl.ds(SUBLANES - back, tm, stride=1), :]
                 for t in range(slabs)], axis=1)
            out = out + wconv_ref[CONV_WIDTH - 1 - back:CONV_WIDTH - back, cols] * shifted
        return out

    acc = jnp.zeros((tm, o_ref.shape[2]), F32)
    pending = tuple((jnp.concatenate([fu[k] for fu in first_up], axis=0), c)
                    for k, c in enumerate(chunk_cols(0)))
    for ci in range(n_chunks):
        (uv, vcols), (ug, gcols) = pending
        if ci + 1 < n_chunks:
            pending = up(ci + 1)
        val = conv(uv, vcols, 0)
        gte = conv(ug, gcols, 1)
        act = val * (gte * (1.0 / (1.0 + jnp.exp(-gte))))
        acc = acc + _dot(act.astype(BF16), wdown_ref[ci * fc:(ci + 1) * fc, :])

    gate2 = mod_ref[0, 5:6, :]
    y = x1_ref[...] + (1.0 + gate2) * acc
    o_ref[0] = _rms_normalise(y, gfin_ref[...])


def _mix_ffn(x, o_sb, o_gla, mod, g2, g_final, w_o, w_up, w_conv, b_conv, w_down):
    b, s, d = x.shape
    tm = ROW_TILE
    const = lambda bi, i: (0, 0)
    row = lambda bi, i: (bi, i, 0)
    return pl.pallas_call(
        _ffn_kernel,
        out_shape=jax.ShapeDtypeStruct((b, s, d), F32),
        grid=(b, s // tm),
        in_specs=[
            pl.BlockSpec((1, tm, d), row),
            pl.BlockSpec((1, tm, SB_WIDTH), row),
            pl.BlockSpec((1, tm, GLA_VWIDTH), row),
            pl.BlockSpec((1, N_MOD, d), lambda bi, i: (bi, 0, 0)),
            pl.BlockSpec((1, d), const),
            pl.BlockSpec((1, d), const),
            pl.BlockSpec(w_o.shape, const, pipeline_mode=pl.Buffered(1)),
            pl.BlockSpec(w_up.shape, const, pipeline_mode=pl.Buffered(1)),
            pl.BlockSpec(w_conv.shape, const),
            pl.BlockSpec(b_conv.shape, const),
            pl.BlockSpec(w_down.shape, const, pipeline_mode=pl.Buffered(1)),
        ],
        out_specs=pl.BlockSpec((1, tm, d), row),
        scratch_shapes=[pltpu.VMEM((SUBLANES, w_up.shape[1]), F32),
                        pltpu.VMEM((2 * FFN_CHUNK // LANES, SUBLANES + tm, LANES), F32),
                        pltpu.VMEM((tm, d), F32)],
        compiler_params=pltpu.CompilerParams(
            dimension_semantics=("arbitrary", "arbitrary"), vmem_limit_bytes=VMEM_LIMIT),
        name="mix_ffn",
    )(x, o_sb, o_gla, mod, g2, g_final, w_o, w_up, w_conv, b_conv, w_down)


def kernel(x, c, w_ada, b_ada, g_norm1, w_in, w_fg2, b_fg2, g_gla_out, w_out,
           g_norm2, w_up, w_conv, b_conv, w_down, g_final):
    depth = w_ada.shape[0]
    bsz, _, d = x.shape
    gf_col = 3 * SB_WIDTH + 2 * GLA_KWIDTH + 2 * GLA_VWIDTH
    out = x
    for l in range(depth):
        c_pad = jnp.zeros((SUBLANES, d), F32).at[:bsz].set(c)
        mod = _adaln(c_pad, w_ada[l], b_ada[l][None, :])[:bsz].reshape(bsz, N_MOD, d)

        assert w_in.shape[2] == gf_col + GLA_GATE_RANK
        w_gf = jnp.pad(w_in[l][:, gf_col:], ((0, 0), (0, LANES - GLA_GATE_RANK))).astype(BF16)
        w_fg = jnp.pad(w_fg2[l], ((0, LANES - GLA_GATE_RANK), (0, 0)))
        gv_col = 3 * SB_WIDTH + 2 * GLA_KWIDTH
        w_gvt = w_in[l][:, gv_col:gv_col + GLA_VWIDTH].T.astype(BF16)
        sbq, sbk, sbv, gq, gk, gv_t, gg, la, w_o, w_u, w_d = _inproj(
            out, mod, g_norm1[l][None, :], w_in[l].astype(BF16), w_gvt, w_gf, w_fg,
            b_fg2[l][None, :], (w_out[l], w_up[l], w_down[l]))

        o_sb = _sb_attention(sbq, sbk, sbv)
        o_gla = _gla(gq, gk, gv_t, la, gg, g_gla_out[l][None, :])

        assert depth == 1
        out = _mix_ffn(out, o_sb, o_gla, mod, g_norm2[l][None, :], g_final[None, :],
                       w_o, w_u, w_conv[l], b_conv[l][None, :], w_d)
    return out
```

```python
import functools

import jax
import jax.numpy as jnp
from jax import lax
from jax.experimental import pallas as pl
from jax.experimental.pallas import tpu as pltpu

F32 = jnp.float32
BF16 = jnp.bfloat16

EPS = 1e-6
SB_HEAD_DIM = 64
SB_WIDTH = 512
GLA_HEADS = 4
GLA_DK = 64
GLA_DV = 128
GLA_KWIDTH = GLA_HEADS * GLA_DK
GLA_VWIDTH = GLA_HEADS * GLA_DV
GLA_GATE_RANK = 16
GLA_TAU = 16.0
CHUNK = 64
CONV_WIDTH = 3
N_MOD = 6

LANES = 128
SUBLANES = 8
VMEM_LIMIT = 56 * 1024 * 1024

ROW_TILE = 512
ADALN_TILE = 768
INPROJ_TILE = 1024
SUB_TILE = 256
SB_BLOCK = 128
SB_DROP_ZERO = 150.0
SB_EXP2_CLAMP = 126.0
LOG2_E = 1.4426950408889634
SB_WINDOW = 3
SB_QUERY_TILE = 512
SB_STAGE_LAG = 3
FFN_CHUNK = 256


def _dot(a, b):
    return jnp.dot(a, b, preferred_element_type=F32)


def _dot_nt(a, b):
    return lax.dot_general(a, b, (((1,), (1,)), ((), ())), preferred_element_type=F32)


def _dot_tn(a, b):
    return lax.dot_general(a, b, (((0,), (0,)), ((), ())), preferred_element_type=F32)


def _split_bf16(a):
    hi = a.astype(BF16)
    lo = (a - hi.astype(F32)).astype(BF16)
    return hi, lo


def _rms_normalise(x, gain):
    ms = jnp.mean(x * x, axis=-1, keepdims=True)
    return x * lax.rsqrt(ms + EPS) * gain


def _adaln_kernel(c_ref, w_ref, b_ref, win_ref, o_ref, win_bf_ref):
    c = c_ref[...]
    a = c * (1.0 / (1.0 + jnp.exp(-c)))
    a_hi, a_lo = _split_bf16(a)
    w_hi, w_lo = _split_bf16(w_ref[...])
    o_ref[...] = (_dot(a_hi, w_hi) + _dot(a_lo, w_hi) + _dot(a_hi, w_lo)) + b_ref[...]
    win_bf_ref[...] = win_ref[...].astype(BF16)


def _adaln(c_pad, w_ada, b_ada, w_in):
    rows, d = c_pad.shape
    n = w_ada.shape[1]
    tn = ADALN_TILE
    steps = n // tn
    slab_rows = w_in.shape[0] // steps
    assert steps * tn == n and slab_rows * steps == w_in.shape[0]
    assert slab_rows % (2 * SUBLANES) == 0
    slab_spec = pl.BlockSpec((slab_rows, w_in.shape[1]), lambda j: (j, 0))
    return pl.pallas_call(
        _adaln_kernel,
        out_shape=[jax.ShapeDtypeStruct((rows, n), F32),
                   jax.ShapeDtypeStruct(w_in.shape, BF16)],
        grid=(steps,),
        in_specs=[
            pl.BlockSpec((rows, d), lambda j: (0, 0)),
            pl.BlockSpec((d, tn), lambda j: (0, j)),
            pl.BlockSpec((1, tn), lambda j: (0, j)),
            slab_spec,
        ],
        out_specs=[pl.BlockSpec((rows, tn), lambda j: (0, j)), slab_spec],
        compiler_params=pltpu.CompilerParams(
            dimension_semantics=("arbitrary",), vmem_limit_bytes=VMEM_LIMIT),
        name="adaln",
    )(c_pad, w_ada, b_ada, w_in)


def _inproj_kernel(x_ref, mod_ref, g_ref, win_ref, wgvt_ref, wgf_ref, wfg_ref, bfg_ref,
                   wout_ref, wup_ref, wdown_ref,
                   sbq_ref, sbk_ref, sbv_ref, gq_ref, gk_ref, gvt_ref, gg_ref, la_ref,
                   wout_bf_ref, wup_bf_ref, wdown_bf_ref):
    wout_bf_ref[...] = wout_ref[...].astype(BF16)
    wup_bf_ref[...] = wup_ref[...].astype(BF16)
    wdown_bf_ref[...] = wdown_ref[...].astype(BF16)

    tm = x_ref.shape[1]
    sub = SUB_TILE
    shift = mod_ref[0, 0:1, :]
    scale = mod_ref[0, 1:2, :]
    gain = g_ref[...]
    wfg_hi, wfg_lo = _split_bf16(wfg_ref[...])
    w = SB_WIDTH
    kw, vw = GLA_KWIDTH, GLA_VWIDTH
    o = 3 * w

    def normalise(si):
        x = x_ref[0, si * sub:(si + 1) * sub, :]
        return (_rms_normalise(x, gain) * (1.0 + scale) + shift).astype(BF16)

    def project(hb, si):
        rows = slice(si * sub, (si + 1) * sub)
        sbq_ref[0, rows, :] = (_dot(hb, win_ref[:, 0:w])
                               * (LOG2_E * SB_HEAD_DIM ** -0.5)).astype(BF16)
        sbk_ref[0, rows, :] = _dot(hb, win_ref[:, w:2 * w]).astype(BF16)
        sbv_ref[0, rows, :] = _dot(hb, win_ref[:, 2 * w:3 * w]).astype(BF16)
        gq_ref[0, rows, :] = (_dot(hb, win_ref[:, o:o + kw]) * (GLA_DK ** -0.5)).astype(BF16)
        gk_ref[0, rows, :] = _dot(hb, win_ref[:, o + kw:o + 2 * kw]).astype(BF16)
        ov = o + 2 * kw
        gvt_ref[0, :, rows] = _dot_nt(wgvt_ref[...], hb).astype(BF16)
        gg_ref[0, rows, :] = _dot(hb, win_ref[:, ov + vw:ov + 2 * vw]).astype(BF16)
        gf = _dot(hb, wgf_ref[...])
        gf_hi, gf_lo = _split_bf16(gf)
        pre = (_dot(gf_hi, wfg_hi) + _dot(gf_lo, wfg_hi) + _dot(gf_hi, wfg_lo)) + bfg_ref[...]
        log_sig = jnp.minimum(pre, 0.0) - jnp.log(1.0 + jnp.exp(-jnp.abs(pre)))
        la_ref[0, rows, :] = log_sig * (1.0 / GLA_TAU)

    pending = normalise(0)
    for si in range(tm // sub):
        hb = pending
        if si + 1 < tm // sub:
            pending = normalise(si + 1)
        project(hb, si)


def _inproj(x, mod, g1, w_in, w_gvt, w_gf, w_fg, b_fg, later_weights):
    b, s, d = x.shape
    tm = INPROJ_TILE
    steps = b * (s // tm)
    const = lambda bi, i: (0, 0)
    row = lambda bi, i: (bi, i, 0)
    slab = lambda bi, i: (bi * (s // tm) + i, 0)
    widths = (SB_WIDTH, SB_WIDTH, SB_WIDTH, GLA_KWIDTH, GLA_KWIDTH, None, GLA_VWIDTH)
    out_shape, out_specs = [], []
    for wd in widths:
        if wd is None:
            out_shape.append(jax.ShapeDtypeStruct((b, GLA_VWIDTH, s), BF16))
            out_specs.append(pl.BlockSpec((1, GLA_VWIDTH, tm), lambda bi, i: (bi, 0, i)))
        else:
            out_shape.append(jax.ShapeDtypeStruct((b, s, wd), BF16))
            out_specs.append(pl.BlockSpec((1, tm, wd), row))
    out_shape.append(jax.ShapeDtypeStruct((b, s, GLA_KWIDTH), F32))
    out_specs.append(pl.BlockSpec((1, tm, GLA_KWIDTH), row))
    weight_specs = []
    for wgt in later_weights:
        rows, cols = wgt.shape
        slab_rows = rows // steps
        assert slab_rows * steps == rows and slab_rows % (2 * SUBLANES) == 0
        weight_specs.append(pl.BlockSpec((slab_rows, cols), slab))
        out_shape.append(jax.ShapeDtypeStruct(wgt.shape, BF16))
    out_specs += weight_specs
    return pl.pallas_call(
        _inproj_kernel,
        out_shape=out_shape,
        grid=(b, s // tm),
        in_specs=[
            pl.BlockSpec((1, tm, d), row),
            pl.BlockSpec((1, N_MOD, d), lambda bi, i: (bi, 0, 0)),
            pl.BlockSpec((1, d), const),
            pl.BlockSpec(w_in.shape, const, pipeline_mode=pl.Buffered(1)),
            pl.BlockSpec(w_gvt.shape, const),
            pl.BlockSpec(w_gf.shape, const),
            pl.BlockSpec(w_fg.shape, const),
            pl.BlockSpec(b_fg.shape, const),
        ] + weight_specs,
        out_specs=out_specs,
        compiler_params=pltpu.CompilerParams(
            dimension_semantics=("arbitrary", "arbitrary"), vmem_limit_bytes=VMEM_LIMIT),
        name="inproj",
    )(x, mod, g1, w_in, w_gvt, w_gf, w_fg, b_fg, *later_weights)


def _sb_kernel(q_ref, k_ref, v_ref, o_ref, acc_ref, run_ref):
    blk = SB_BLOCK
    pairs = q_ref.shape[2] // LANES
    lane = lax.broadcasted_iota(jnp.int32, (1, LANES), 1)
    first = lane < SB_HEAD_DIM

    kj = lax.broadcasted_iota(jnp.int32, (2 * blk, 2 * blk), 0) & (blk - 1)
    ks = lax.broadcasted_iota(jnp.int32, (2 * blk, 2 * blk), 1)
    tri = jnp.where((kj >= ks) | (ks >= blk), 1.0, 0.0).astype(BF16)

    t_pos = lax.broadcasted_iota(jnp.int32, (blk, blk), 0)
    s_pos = lax.broadcasted_iota(jnp.int32, (blk, blk), 1)
    causal = s_pos < t_pos

    heads = range(2 * pairs)

    def key_blocks(q_pairs, kbs, fresh):
        starts = [pl.multiple_of(kb * blk, blk) for kb in kbs]
        streams = [(j, p) for j in range(len(kbs)) for p in range(pairs)]
        run = [None if fresh else run_ref[h] for h in heads]
        acc = [None if fresh else acc_ref[:, p * LANES:(p + 1) * LANES] for p in range(pairs)]
        scores, sums = {}, {}

        def stage_scores(j, p):
            k = k_ref[0, pl.ds(starts[j], blk), p * LANES:(p + 1) * LANES]
            zk = jnp.zeros_like(k)
            k_heads = jnp.concatenate([jnp.where(first, k, zk), jnp.where(first, zk, k)], axis=0)
            both = _dot_nt(q_pairs[p], k_heads)
            scores[j, p] = (both[:, :blk], both[:, blk:])

        def stage_sums(j, p):
            out = []
            for z in scores[j, p]:
                d = jnp.maximum(z, jnp.log2(1.0 + jnp.exp2(jnp.minimum(z, SB_EXP2_CLAMP))))
                if fresh and j == 0:
                    d = jnp.where(causal, d, 0.0)
                hi, lo = _split_bf16(d)
                out.append(_dot(jnp.concatenate([hi, lo], axis=1), tri))
            sums[j, p] = out

        def stage_weights(j, p):
            weights = []
            for e, h in enumerate((2 * p, 2 * p + 1)):
                arg = scores[j, p][e] - sums[j, p][e][:, :blk]
                wgt = jnp.exp2(arg if run[h] is None else arg - run[h])
                if fresh and j == 0:
                    wgt = jnp.where(causal, wgt, 0.0)
                total = sums[j, p][e][:, blk:]
                run[h] = total if run[h] is None else run[h] + total
                weights.append(wgt.astype(BF16))
            v = v_ref[0, pl.ds(starts[j], blk), p * LANES:(p + 1) * LANES]
            zv = jnp.zeros_like(v)
            v_heads = jnp.concatenate([jnp.where(first, v, zv), jnp.where(first, zv, v)], axis=0)
            out = _dot(jnp.concatenate(weights, axis=1), v_heads)
            acc[p] = out if acc[p] is None else acc[p] + out

        lag = min(SB_STAGE_LAG, len(streams))
        for t in range(len(streams) + 2 * lag):
            if t < len(streams):
                stage_scores(*streams[t])
            if lag <= t < len(streams) + lag:
                stage_sums(*streams[t - lag])
            if t >= 2 * lag:
                stage_weights(*streams[t - 2 * lag])
        for h in heads:
            run_ref[h] = run[h]
        for p in range(pairs):
            acc_ref[:, p * LANES:(p + 1) * LANES] = acc[p]

    window = SB_WINDOW
    n_sub = q_ref.shape[1] // blk

    def least_drop():
        least = run_ref[0]
        for h in heads[1:]:
            least = jnp.minimum(least, run_ref[h])
        return jnp.min(least)

    def query_block(sub, carry):
        i = pl.program_id(1) * n_sub + sub
        rows = pl.ds(pl.multiple_of(sub * blk, blk), blk)
        q_pairs = [q_ref[0, rows, p * LANES:(p + 1) * LANES] for p in range(pairs)]

        @pl.when(i >= window - 1)
        def _():
            key_blocks(q_pairs, [i - j for j in range(window)], True)

        @pl.when(i < window - 1)
        def _():
            key_blocks(q_pairs, [i], True)

        def cond(state):
            kb, least = state
            return jnp.logical_and(kb >= 0, least < SB_DROP_ZERO)

        def body(state):
            kb, _ = state
            key_blocks(q_pairs, [kb], False)
            return kb - 1, least_drop()

        lax.while_loop(cond, body, (jnp.where(i >= window - 1, i - window, i - 1), least_drop()))
        o_ref[0, rows, :] = acc_ref[...].astype(BF16)
        return carry

    lax.fori_loop(0, n_sub, query_block, 0)


def _sb_attention(q, k, v):
    b, s, w = q.shape
    blk = SB_BLOCK
    heads = w // SB_HEAD_DIM
    kv_spec = pl.BlockSpec((1, s, w), lambda bi, i: (bi, 0, 0), pipeline_mode=pl.Buffered(1))
    tq = SB_QUERY_TILE
    q_spec = pl.BlockSpec((1, tq, w), lambda bi, i: (bi, i, 0))
    return pl.pallas_call(
        _sb_kernel,
        out_shape=jax.ShapeDtypeStruct((b, s, w), BF16),
        grid=(b, s // tq),
        in_specs=[q_spec, kv_spec, kv_spec],
        out_specs=q_spec,
        scratch_shapes=[pltpu.VMEM((blk, w), F32), pltpu.VMEM((heads, blk, blk), F32)],
        compiler_params=pltpu.CompilerParams(
            dimension_semantics=("arbitrary", "arbitrary"),
            vmem_limit_bytes=VMEM_LIMIT),
        name="sb_attn",
    )(q, k, v)


def _gla_kernel(q_ref, k_ref, vt_ref, la_ref, gate_ref, gain_ref, later_ref, member_ref,
                o_ref, state_ref, o_scr):
    tm = q_ref.shape[1]
    n_chunks = tm // CHUNK

    @pl.when(pl.program_id(1) == 0)
    def _():
        state_ref[...] = jnp.zeros_like(state_ref)

    la_hi, la_lo = _split_bf16(la_ref[0])

    later = later_ref[...]
    rev = _dot(later, la_hi) + _dot(later, la_lo)
    k_dec = (k_ref[0].astype(F32) * jnp.exp(rev)).astype(BF16)

    member = member_ref[...]
    decay = jnp.exp(_dot(member, la_hi) + _dot(member, la_lo))

    pairs = GLA_HEADS // 2
    pr = lax.broadcasted_iota(jnp.int32, (2 * GLA_DV, LANES), 0)
    pc = lax.broadcasted_iota(jnp.int32, (2 * GLA_DV, LANES), 1)
    own_head = (pr >= GLA_DV) == (pc >= GLA_DK)

    q = q_ref[0]
    zero_block = jnp.zeros((2 * GLA_DV, LANES), BF16)
    per_tile = LANES // CHUNK
    frame = lax.broadcasted_iota(jnp.int32, (LANES, 1), 0)
    updates = []
    for ci in range(n_chunks):
        tile = slice((ci // per_tile) * LANES, (ci // per_tile + 1) * LANES)
        in_chunk = (frame >= (ci % per_tile) * CHUNK) & (frame < (ci % per_tile + 1) * CHUNK)
        k_tile = k_dec[tile]
        k_tile = jnp.where(in_chunk, k_tile, jnp.zeros_like(k_tile))
        updates.append([
            jnp.where(own_head,
                      _dot(vt_ref[0, p * 2 * GLA_DV:(p + 1) * 2 * GLA_DV, tile],
                           k_tile[:, p * LANES:(p + 1) * LANES]), 0.0)
            for p in range(pairs)])

    states = [state_ref[p] for p in range(pairs)]
    readouts = []
    for ci in range(n_chunks):
        block_rows = []
        for p in range(pairs):
            lanes = slice(p * LANES, (p + 1) * LANES)
            states[p] = decay[ci:ci + 1, lanes] * states[p] + updates[ci][p]
            parts = [zero_block] * pairs
            parts[p] = states[p].astype(BF16)
            block_rows.append(jnp.concatenate(parts, axis=1))
        readouts.append(jnp.concatenate(block_rows, axis=0))
    for p in range(pairs):
        state_ref[p] = states[p]
    for ci in range(n_chunks):
        rows = slice(ci * CHUNK, (ci + 1) * CHUNK)
        o_scr[rows, :] = _dot_nt(q[rows], readouts[ci])

    gain = gain_ref[...]
    gate = gate_ref[0].astype(F32)
    swish = gate * (1.0 / (1.0 + jnp.exp(-gate)))
    for h in range(GLA_HEADS):
        cols = slice(h * GLA_DV, (h + 1) * GLA_DV)
        oh = o_scr[:, cols]
        ms = jnp.mean(oh * oh, axis=-1, keepdims=True)
        o_ref[0, :, cols] = (oh * lax.rsqrt(ms + EPS) * gain[:, cols] * swish[:, cols]).astype(BF16)


def _gla(q, k, v_t, la, gate, gain):
    b, s, _ = q.shape
    tm = ROW_TILE
    row = lambda bi, i: (bi, i, 0)
    const = lambda bi, i: (0, 0)
    n_chunks = tm // CHUNK
    t_idx = jnp.arange(tm)
    later = ((t_idx[:, None] // CHUNK == t_idx[None, :] // CHUNK)
             & (t_idx[None, :] > t_idx[:, None])).astype(BF16)
    member = (jnp.arange(2 * SUBLANES)[:, None] == t_idx[None, :] // CHUNK).astype(BF16)
    assert n_chunks <= 2 * SUBLANES
    return pl.pallas_call(
        _gla_kernel,
        out_shape=jax.ShapeDtypeStruct((b, s, GLA_VWIDTH), BF16),
        grid=(b, s // tm),
        in_specs=[
            pl.BlockSpec((1, tm, GLA_KWIDTH), row),
            pl.BlockSpec((1, tm, GLA_KWIDTH), row),
            pl.BlockSpec((1, GLA_VWIDTH, tm), lambda bi, i: (bi, 0, i)),
            pl.BlockSpec((1, tm, GLA_KWIDTH), row),
            pl.BlockSpec((1, tm, GLA_VWIDTH), row),
            pl.BlockSpec((1, GLA_VWIDTH), const),
            pl.BlockSpec(later.shape, const),
            pl.BlockSpec(member.shape, const),
        ],
        out_specs=pl.BlockSpec((1, tm, GLA_VWIDTH), row),
        scratch_shapes=[pltpu.VMEM((GLA_HEADS // 2, 2 * GLA_DV, 2 * GLA_DK), F32),
                        pltpu.VMEM((tm, GLA_VWIDTH), F32)],
        compiler_params=pltpu.CompilerParams(
            dimension_semantics=("arbitrary", "arbitrary"), vmem_limit_bytes=VMEM_LIMIT),
        name="gla",
    )(q, k, v_t, la, gate, gain, later, member)


def _ffn_kernel(x_ref, osb_ref, ogla_ref, mod_ref, g2_ref, gfin_ref, wout_ref, wup_ref,
                wconv_ref, bconv_ref, wdown_ref, o_ref, tail_ref, stage_ref, x1_ref):
    tm = x_ref.shape[1]
    d_ff = wdown_ref.shape[0]
    fc = FFN_CHUNK
    n_chunks = d_ff // fc

    @pl.when(pl.program_id(1) == 0)
    def _():
        tail_ref[...] = jnp.zeros_like(tail_ref)

    gate1 = mod_ref[0, 2:3, :]
    shift2 = mod_ref[0, 3:4, :]
    scale2 = mod_ref[0, 4:5, :]
    gain2 = g2_ref[...]

    def chunk_cols(ci):
        return (slice(ci * fc, (ci + 1) * fc), slice(d_ff + ci * fc, d_ff + (ci + 1) * fc))

    def mix(rows):
        mixed = (_dot(osb_ref[0, rows, :], wout_ref[0:SB_WIDTH, :])
                 + _dot(ogla_ref[0, rows, :], wout_ref[SB_WIDTH:SB_WIDTH + GLA_VWIDTH, :]))
        x1 = x_ref[0, rows, :] + (1.0 + gate1) * mixed
        x1_ref[rows, :] = x1
        return (_rms_normalise(x1, gain2) * (1.0 + scale2) + shift2).astype(BF16)

    halves = (slice(0, tm // 2), slice(tm // 2, tm))
    h_parts, first_up = [], []
    for rows in halves:
        hp = mix(rows)
        h_parts.append(hp)
        first_up.append([_dot(hp, wup_ref[:, c]) for c in chunk_cols(0)])
    h = jnp.concatenate(h_parts, axis=0)

    def up(ci):
        return tuple((_dot(h, wup_ref[:, c]), c) for c in chunk_cols(ci))

    def conv(u, cols, slot):
        tail = tail_ref[:, cols]
        slabs = fc // LANES
        for t in range(slabs):
            lanes = slice(t * LANES, (t + 1) * LANES)
            stage_ref[slot * slabs + t, 0:SUBLANES, :] = tail[:, lanes]
            stage_ref[slot * slabs + t, SUBLANES:, :] = u[:, lanes]
        tail_ref[:, cols] = u[tm - SUBLANES:, :]
        out = bconv_ref[:, cols] + wconv_ref[CONV_WIDTH - 1:CONV_WIDTH, cols] * u
        for back in range(1, CONV_WIDTH):
            shifted = jnp.concatenate(
                [stage_ref[slot * slabs + t, pl.ds(SUBLANES - back, tm, stride=1), :]
                 for t in range(slabs)], axis=1)
            out = out + wconv_ref[CONV_WIDTH - 1 - back:CONV_WIDTH - back, cols] * shifted
        return out

    acc = jnp.zeros((tm, o_ref.shape[2]), F32)
    pending = tuple((jnp.concatenate([fu[k] for fu in first_up], axis=0), c)
                    for k, c in enumerate(chunk_cols(0)))
    for ci in range(n_chunks):
        (uv, vcols), (ug, gcols) = pending
        if ci + 1 < n_chunks:
            pending = up(ci + 1)
        val = conv(uv, vcols, 0)
        gte = conv(ug, gcols, 1)
        act = val * (gte * (1.0 / (1.0 + jnp.exp(-gte))))
        acc = acc + _dot(act.astype(BF16), wdown_ref[ci * fc:(ci + 1) * fc, :])

    gate2 = mod_ref[0, 5:6, :]
    y = x1_ref[...] + (1.0 + gate2) * acc
    o_ref[0] = _rms_normalise(y, gfin_ref[...])


def _mix_ffn(x, o_sb, o_gla, mod, g2, g_final, w_o, w_up, w_conv, b_conv, w_down):
    b, s, d = x.shape
    tm = ROW_TILE
    const = lambda bi, i: (0, 0)
    row = lambda bi, i: (bi, i, 0)
    return pl.pallas_call(
        _ffn_kernel,
        out_shape=jax.ShapeDtypeStruct((b, s, d), F32),
        grid=(b, s // tm),
        in_specs=[
            pl.BlockSpec((1, tm, d), row),
            pl.BlockSpec((1, tm, SB_WIDTH), row),
            pl.BlockSpec((1, tm, GLA_VWIDTH), row),
            pl.BlockSpec((1, N_MOD, d), lambda bi, i: (bi, 0, 0)),
            pl.BlockSpec((1, d), const),
            pl.BlockSpec((1, d), const),
            pl.BlockSpec(w_o.shape, const, pipeline_mode=pl.Buffered(1)),
            pl.BlockSpec(w_up.shape, const, pipeline_mode=pl.Buffered(1)),
            pl.BlockSpec(w_conv.shape, const),
            pl.BlockSpec(b_conv.shape, const),
            pl.BlockSpec(w_down.shape, const, pipeline_mode=pl.Buffered(1)),
        ],
        out_specs=pl.BlockSpec((1, tm, d), row),
        scratch_shapes=[pltpu.VMEM((SUBLANES, w_up.shape[1]), F32),
                        pltpu.VMEM((2 * FFN_CHUNK // LANES, SUBLANES + tm, LANES), F32),
                        pltpu.VMEM((tm, d), F32)],
        compiler_params=pltpu.CompilerParams(
            dimension_semantics=("arbitrary", "arbitrary"), vmem_limit_bytes=VMEM_LIMIT),
        name="mix_ffn",
    )(x, o_sb, o_gla, mod, g2, g_final, w_o, w_up, w_conv, b_conv, w_down)


def kernel(x, c, w_ada, b_ada, g_norm1, w_in, w_fg2, b_fg2, g_gla_out, w_out,
           g_norm2, w_up, w_conv, b_conv, w_down, g_final):
    depth = w_ada.shape[0]
    bsz, _, d = x.shape
    gf_col = 3 * SB_WIDTH + 2 * GLA_KWIDTH + 2 * GLA_VWIDTH
    out = x
    for l in range(depth):
        c_pad = jnp.zeros((SUBLANES, d), F32).at[:bsz].set(c)
        mod, w_in_bf = _adaln(c_pad, w_ada[l], b_ada[l][None, :], w_in[l])
        mod = mod[:bsz].reshape(bsz, N_MOD, d)

        assert w_in.shape[2] == gf_col + GLA_GATE_RANK
        w_gf = jnp.pad(w_in[l][:, gf_col:], ((0, 0), (0, LANES - GLA_GATE_RANK))).astype(BF16)
        w_fg = jnp.pad(w_fg2[l], ((0, LANES - GLA_GATE_RANK), (0, 0)))
        gv_col = 3 * SB_WIDTH + 2 * GLA_KWIDTH
        w_gvt = w_in[l][:, gv_col:gv_col + GLA_VWIDTH].T.astype(BF16)
        sbq, sbk, sbv, gq, gk, gv_t, gg, la, w_o, w_u, w_d = _inproj(
            out, mod, g_norm1[l][None, :], w_in_bf, w_gvt, w_gf, w_fg,
            b_fg2[l][None, :], (w_out[l], w_up[l], w_down[l]))

        o_sb = _sb_attention(sbq, sbk, sbv)
        o_gla = _gla(gq, gk, gv_t, la, gg, g_gla_out[l][None, :])

        assert depth == 1
        out = _mix_ffn(out, o_sb, o_gla, mod, g_norm2[l][None, :], g_final[None, :],
                       w_o, w_u, w_conv[l], b_conv[l][None, :], w_d)
    return out
```

```python
import functools

import jax
import jax.numpy as jnp
from jax import lax
from jax.experimental import pallas as pl
from jax.experimental.pallas import tpu as pltpu

F32 = jnp.float32
BF16 = jnp.bfloat16

EPS = 1e-6
SB_HEAD_DIM = 64
SB_WIDTH = 512
GLA_HEADS = 4
GLA_DK = 64
GLA_DV = 128
GLA_KWIDTH = GLA_HEADS * GLA_DK
GLA_VWIDTH = GLA_HEADS * GLA_DV
GLA_GATE_RANK = 16
GLA_TAU = 16.0
CHUNK = 64
CONV_WIDTH = 3
N_MOD = 6

LANES = 128
SUBLANES = 8
VMEM_LIMIT = 56 * 1024 * 1024

ROW_TILE = 512
ADALN_TILE = 1024
INPROJ_TILE = 1024
SUB_TILE = 256
SB_BLOCK = 128
SB_DROP_ZERO = 150.0
SB_EXP2_CLAMP = 126.0
LOG2_E = 1.4426950408889634
SB_WINDOW = 3
SB_QUERY_TILE = 512
SB_STAGE_LAG = 3
FFN_CHUNK = 256


def _dot(a, b):
    return jnp.dot(a, b, preferred_element_type=F32)


def _dot_nt(a, b):
    return lax.dot_general(a, b, (((1,), (1,)), ((), ())), preferred_element_type=F32)


def _dot_tn(a, b):
    return lax.dot_general(a, b, (((0,), (0,)), ((), ())), preferred_element_type=F32)


def _split_bf16(a):
    hi = a.astype(BF16)
    lo = (a - hi.astype(F32)).astype(BF16)
    return hi, lo


def _rms_normalise(x, gain):
    ms = jnp.mean(x * x, axis=-1, keepdims=True)
    return x * lax.rsqrt(ms + EPS) * gain


def _adaln_kernel(c_ref, w_ref, b_ref, o_ref):
    c = c_ref[...]
    a = c * (1.0 / (1.0 + jnp.exp(-c)))
    a_hi, a_lo = _split_bf16(a)
    w_hi, w_lo = _split_bf16(w_ref[...])
    o_ref[...] = (_dot(a_hi, w_hi) + _dot(a_lo, w_hi) + _dot(a_hi, w_lo)) + b_ref[...]


def _adaln(c_pad, w_ada, b_ada):
    rows, d = c_pad.shape
    n = w_ada.shape[1]
    tn = ADALN_TILE
    return pl.pallas_call(
        _adaln_kernel,
        out_shape=jax.ShapeDtypeStruct((rows, n), F32),
        grid=(n // tn,),
        in_specs=[
            pl.BlockSpec((rows, d), lambda j: (0, 0)),
            pl.BlockSpec((d, tn), lambda j: (0, j)),
            pl.BlockSpec((1, tn), lambda j: (0, j)),
        ],
        out_specs=pl.BlockSpec((rows, tn), lambda j: (0, j)),
        compiler_params=pltpu.CompilerParams(
            dimension_semantics=("arbitrary",), vmem_limit_bytes=VMEM_LIMIT),
        name="adaln",
    )(c_pad, w_ada, b_ada)


def _inproj_kernel(x_ref, mod_ref, g_ref, win_ref, wgvt_ref, wgf_ref, wfg_ref, bfg_ref,
                   wout_ref, wup_ref, wdown_ref,
                   sbq_ref, sbk_ref, sbv_ref, gq_ref, gk_ref, gvt_ref, gg_ref, la_ref,
                   wout_bf_ref, wup_bf_ref, wdown_bf_ref):
    wout_bf_ref[...] = wout_ref[...].astype(BF16)
    wup_bf_ref[...] = wup_ref[...].astype(BF16)
    wdown_bf_ref[...] = wdown_ref[...].astype(BF16)

    tm = x_ref.shape[1]
    sub = SUB_TILE
    shift = mod_ref[0, 0:1, :]
    scale = mod_ref[0, 1:2, :]
    gain = g_ref[...]
    wfg_hi, wfg_lo = _split_bf16(wfg_ref[...])
    w = SB_WIDTH
    kw, vw = GLA_KWIDTH, GLA_VWIDTH
    o = 3 * w

    def normalise(si):
        x = x_ref[0, si * sub:(si + 1) * sub, :]
        return (_rms_normalise(x, gain) * (1.0 + scale) + shift).astype(BF16)

    def project(hb, si):
        rows = slice(si * sub, (si + 1) * sub)
        sbq_ref[0, rows, :] = (_dot(hb, win_ref[:, 0:w])
                               * (LOG2_E * SB_HEAD_DIM ** -0.5)).astype(BF16)
        sbk_ref[0, rows, :] = _dot(hb, win_ref[:, w:2 * w]).astype(BF16)
        sbv_ref[0, rows, :] = _dot(hb, win_ref[:, 2 * w:3 * w]).astype(BF16)
        gq_ref[0, rows, :] = (_dot(hb, win_ref[:, o:o + kw]) * (GLA_DK ** -0.5)).astype(BF16)
        gk_ref[0, rows, :] = _dot(hb, win_ref[:, o + kw:o + 2 * kw]).astype(BF16)
        ov = o + 2 * kw
        gvt_ref[0, :, rows] = _dot_nt(wgvt_ref[...], hb).astype(BF16)
        gg_ref[0, rows, :] = _dot(hb, win_ref[:, ov + vw:ov + 2 * vw]).astype(BF16)
        gf = _dot(hb, wgf_ref[...])
        gf_hi, gf_lo = _split_bf16(gf)
        pre = (_dot(gf_hi, wfg_hi) + _dot(gf_lo, wfg_hi) + _dot(gf_hi, wfg_lo)) + bfg_ref[...]
        log_sig = jnp.minimum(pre, 0.0) - jnp.log(1.0 + jnp.exp(-jnp.abs(pre)))
        la_ref[0, rows, :] = log_sig * (1.0 / GLA_TAU)

    pending = normalise(0)
    for si in range(tm // sub):
        hb = pending
        if si + 1 < tm // sub:
            pending = normalise(si + 1)
        project(hb, si)


def _inproj(x, mod, g1, w_in, w_gvt, w_gf, w_fg, b_fg, later_weights):
    b, s, d = x.shape
    tm = INPROJ_TILE
    steps = b * (s // tm)
    const = lambda bi, i: (0, 0)
    row = lambda bi, i: (bi, i, 0)
    slab = lambda bi, i: (bi * (s // tm) + i, 0)
    widths = (SB_WIDTH, SB_WIDTH, SB_WIDTH, GLA_KWIDTH, GLA_KWIDTH, None, GLA_VWIDTH)
    out_shape, out_specs = [], []
    for wd in widths:
        if wd is None:
            out_shape.append(jax.ShapeDtypeStruct((b, GLA_VWIDTH, s), BF16))
            out_specs.append(pl.BlockSpec((1, GLA_VWIDTH, tm), lambda bi, i: (bi, 0, i)))
        else:
            out_shape.append(jax.ShapeDtypeStruct((b, s, wd), BF16))
            out_specs.append(pl.BlockSpec((1, tm, wd), row))
    out_shape.append(jax.ShapeDtypeStruct((b, s, GLA_KWIDTH), F32))
    out_specs.append(pl.BlockSpec((1, tm, GLA_KWIDTH), row))
    weight_specs = []
    for wgt in later_weights:
        rows, cols = wgt.shape
        slab_rows = rows // steps
        assert slab_rows * steps == rows and slab_rows % (2 * SUBLANES) == 0
        weight_specs.append(pl.BlockSpec((slab_rows, cols), slab))
        out_shape.append(jax.ShapeDtypeStruct(wgt.shape, BF16))
    out_specs += weight_specs
    return pl.pallas_call(
        _inproj_kernel,
        out_shape=out_shape,
        grid=(b, s // tm),
        in_specs=[
            pl.BlockSpec((1, tm, d), row),
            pl.BlockSpec((1, N_MOD, d), lambda bi, i: (bi, 0, 0)),
            pl.BlockSpec((1, d), const),
            pl.BlockSpec(w_in.shape, const, pipeline_mode=pl.Buffered(1)),
            pl.BlockSpec(w_gvt.shape, const),
            pl.BlockSpec(w_gf.shape, const),
            pl.BlockSpec(w_fg.shape, const),
            pl.BlockSpec(b_fg.shape, const),
        ] + weight_specs,
        out_specs=out_specs,
        compiler_params=pltpu.CompilerParams(
            dimension_semantics=("arbitrary", "arbitrary"), vmem_limit_bytes=VMEM_LIMIT),
        name="inproj",
    )(x, mod, g1, w_in, w_gvt, w_gf, w_fg, b_fg, *later_weights)


def _sb_kernel(q_ref, k_ref, v_ref, o_ref, acc_ref, run_ref):
    blk = SB_BLOCK
    pairs = q_ref.shape[2] // LANES
    lane = lax.broadcasted_iota(jnp.int32, (1, LANES), 1)
    first = lane < SB_HEAD_DIM

    kj = lax.broadcasted_iota(jnp.int32, (2 * blk, 2 * blk), 0) & (blk - 1)
    ks = lax.broadcasted_iota(jnp.int32, (2 * blk, 2 * blk), 1)
    tri = jnp.where((kj >= ks) | (ks >= blk), 1.0, 0.0).astype(BF16)

    t_pos = lax.broadcasted_iota(jnp.int32, (blk, blk), 0)
    s_pos = lax.broadcasted_iota(jnp.int32, (blk, blk), 1)
    causal = s_pos < t_pos

    heads = range(2 * pairs)

    def key_blocks(q_pairs, kbs, fresh):
        starts = [pl.multiple_of(kb * blk, blk) for kb in kbs]
        streams = [(j, p) for j in range(len(kbs)) for p in range(pairs)]
        run = [None if fresh else run_ref[h] for h in heads]
        acc = [None if fresh else acc_ref[:, p * LANES:(p + 1) * LANES] for p in range(pairs)]
        scores, sums = {}, {}

        def stage_scores(j, p):
            k = k_ref[0, pl.ds(starts[j], blk), p * LANES:(p + 1) * LANES]
            zk = jnp.zeros_like(k)
            k_heads = jnp.concatenate([jnp.where(first, k, zk), jnp.where(first, zk, k)], axis=0)
            both = _dot_nt(q_pairs[p], k_heads)
            scores[j, p] = (both[:, :blk], both[:, blk:])

        def stage_sums(j, p):
            out = []
            for z in scores[j, p]:
                d = jnp.maximum(z, jnp.log2(1.0 + jnp.exp2(jnp.minimum(z, SB_EXP2_CLAMP))))
                if fresh and j == 0:
                    d = jnp.where(causal, d, 0.0)
                hi, lo = _split_bf16(d)
                out.append(_dot(jnp.concatenate([hi, lo], axis=1), tri))
            sums[j, p] = out

        def stage_weights(j, p):
            weights = []
            for e, h in enumerate((2 * p, 2 * p + 1)):
                arg = scores[j, p][e] - sums[j, p][e][:, :blk]
                wgt = jnp.exp2(arg if run[h] is None else arg - run[h])
                if fresh and j == 0:
                    wgt = jnp.where(causal, wgt, 0.0)
                total = sums[j, p][e][:, blk:]
                run[h] = total if run[h] is None else run[h] + total
                weights.append(wgt.astype(BF16))
            v = v_ref[0, pl.ds(starts[j], blk), p * LANES:(p + 1) * LANES]
            zv = jnp.zeros_like(v)
            v_heads = jnp.concatenate([jnp.where(first, v, zv), jnp.where(first, zv, v)], axis=0)
            out = _dot(jnp.concatenate(weights, axis=1), v_heads)
            acc[p] = out if acc[p] is None else acc[p] + out

        lag = min(SB_STAGE_LAG, len(streams))
        for t in range(len(streams) + 2 * lag):
            if t < len(streams):
                stage_scores(*streams[t])
            if lag <= t < len(streams) + lag:
                stage_sums(*streams[t - lag])
            if t >= 2 * lag:
                stage_weights(*streams[t - 2 * lag])
        for h in heads:
            run_ref[h] = run[h]
        for p in range(pairs):
            acc_ref[:, p * LANES:(p + 1) * LANES] = acc[p]

    window = SB_WINDOW
    n_sub = q_ref.shape[1] // blk

    def least_drop():
        least = run_ref[0]
        for h in heads[1:]:
            least = jnp.minimum(least, run_ref[h])
        return jnp.min(least)

    def query_block(sub, carry):
        i = pl.program_id(1) * n_sub + sub
        rows = pl.ds(pl.multiple_of(sub * blk, blk), blk)
        q_pairs = [q_ref[0, rows, p * LANES:(p + 1) * LANES] for p in range(pairs)]

        @pl.when(i >= window - 1)
        def _():
            key_blocks(q_pairs, [i - j for j in range(window)], True)

        @pl.when(i < window - 1)
        def _():
            key_blocks(q_pairs, [i], True)

        def cond(state):
            kb, least = state
            return jnp.logical_and(kb >= 0, least < SB_DROP_ZERO)

        def body(state):
            kb, _ = state
            key_blocks(q_pairs, [kb], False)
            return kb - 1, least_drop()

        lax.while_loop(cond, body, (jnp.where(i >= window - 1, i - window, i - 1), least_drop()))
        o_ref[0, rows, :] = acc_ref[...].astype(BF16)
        return carry

    lax.fori_loop(0, n_sub, query_block, 0)


def _sb_attention(q, k, v):
    b, s, w = q.shape
    blk = SB_BLOCK
    heads = w // SB_HEAD_DIM
    kv_spec = pl.BlockSpec((1, s, w), lambda bi, i: (bi, 0, 0))
    tq = SB_QUERY_TILE
    q_spec = pl.BlockSpec((1, tq, w), lambda bi, i: (bi, i, 0))
    return pl.pallas_call(
        _sb_kernel,
        out_shape=jax.ShapeDtypeStruct((b, s, w), BF16),
        grid=(b, s // tq),
        in_specs=[q_spec, kv_spec, kv_spec],
        out_specs=q_spec,
        scratch_shapes=[pltpu.VMEM((blk, w), F32), pltpu.VMEM((heads, blk, blk), F32)],
        compiler_params=pltpu.CompilerParams(
            dimension_semantics=("arbitrary", "arbitrary"),
            vmem_limit_bytes=VMEM_LIMIT),
        name="sb_attn",
    )(q, k, v)


def _gla_kernel(q_ref, k_ref, vt_ref, la_ref, gate_ref, gain_ref, later_ref, member_ref,
                o_ref, state_ref, o_scr):
    tm = q_ref.shape[1]
    n_chunks = tm // CHUNK

    @pl.when(pl.program_id(1) == 0)
    def _():
        state_ref[...] = jnp.zeros_like(state_ref)

    la_hi, la_lo = _split_bf16(la_ref[0])

    later = later_ref[...]
    rev = _dot(later, la_hi) + _dot(later, la_lo)
    k_dec = (k_ref[0].astype(F32) * jnp.exp(rev)).astype(BF16)

    member = member_ref[...]
    decay = jnp.exp(_dot(member, la_hi) + _dot(member, la_lo))

    pairs = GLA_HEADS // 2
    pr = lax.broadcasted_iota(jnp.int32, (2 * GLA_DV, LANES), 0)
    pc = lax.broadcasted_iota(jnp.int32, (2 * GLA_DV, LANES), 1)
    own_head = (pr >= GLA_DV) == (pc >= GLA_DK)

    q = q_ref[0]
    zero_block = jnp.zeros((2 * GLA_DV, LANES), BF16)
    per_tile = LANES // CHUNK
    frame = lax.broadcasted_iota(jnp.int32, (LANES, 1), 0)
    updates = []
    for ci in range(n_chunks):
        tile = slice((ci // per_tile) * LANES, (ci // per_tile + 1) * LANES)
        in_chunk = (frame >= (ci % per_tile) * CHUNK) & (frame < (ci % per_tile + 1) * CHUNK)
        k_tile = k_dec[tile]
        k_tile = jnp.where(in_chunk, k_tile, jnp.zeros_like(k_tile))
        updates.append([
            jnp.where(own_head,
                      _dot(vt_ref[0, p * 2 * GLA_DV:(p + 1) * 2 * GLA_DV, tile],
                           k_tile[:, p * LANES:(p + 1) * LANES]), 0.0)
            for p in range(pairs)])

    states = [state_ref[p] for p in range(pairs)]
    readouts = []
    for ci in range(n_chunks):
        block_rows = []
        for p in range(pairs):
            lanes = slice(p * LANES, (p + 1) * LANES)
            states[p] = decay[ci:ci + 1, lanes] * states[p] + updates[ci][p]
            parts = [zero_block] * pairs
            parts[p] = states[p].astype(BF16)
            block_rows.append(jnp.concatenate(parts, axis=1))
        readouts.append(jnp.concatenate(block_rows, axis=0))
    for p in range(pairs):
        state_ref[p] = states[p]
    for ci in range(n_chunks):
        rows = slice(ci * CHUNK, (ci + 1) * CHUNK)
        o_scr[rows, :] = _dot_nt(q[rows], readouts[ci])

    gain = gain_ref[...]
    gate = gate_ref[0].astype(F32)
    swish = gate * (1.0 / (1.0 + jnp.exp(-gate)))
    for h in range(GLA_HEADS):
        cols = slice(h * GLA_DV, (h + 1) * GLA_DV)
        oh = o_scr[:, cols]
        ms = jnp.mean(oh * oh, axis=-1, keepdims=True)
        o_ref[0, :, cols] = (oh * lax.rsqrt(ms + EPS) * gain[:, cols] * swish[:, cols]).astype(BF16)


def _gla(q, k, v_t, la, gate, gain):
    b, s, _ = q.shape
    tm = ROW_TILE
    row = lambda bi, i: (bi, i, 0)
    const = lambda bi, i: (0, 0)
    n_chunks = tm // CHUNK
    t_idx = jnp.arange(tm)
    later = ((t_idx[:, None] // CHUNK == t_idx[None, :] // CHUNK)
             & (t_idx[None, :] > t_idx[:, None])).astype(BF16)
    member = (jnp.arange(2 * SUBLANES)[:, None] == t_idx[None, :] // CHUNK).astype(BF16)
    assert n_chunks <= 2 * SUBLANES
    return pl.pallas_call(
        _gla_kernel,
        out_shape=jax.ShapeDtypeStruct((b, s, GLA_VWIDTH), BF16),
        grid=(b, s // tm),
        in_specs=[
            pl.BlockSpec((1, tm, GLA_KWIDTH), row),
            pl.BlockSpec((1, tm, GLA_KWIDTH), row),
            pl.BlockSpec((1, GLA_VWIDTH, tm), lambda bi, i: (bi, 0, i)),
            pl.BlockSpec((1, tm, GLA_KWIDTH), row),
            pl.BlockSpec((1, tm, GLA_VWIDTH), row),
            pl.BlockSpec((1, GLA_VWIDTH), const),
            pl.BlockSpec(later.shape, const),
            pl.BlockSpec(member.shape, const),
        ],
        out_specs=pl.BlockSpec((1, tm, GLA_VWIDTH), row),
        scratch_shapes=[pltpu.VMEM((GLA_HEADS // 2, 2 * GLA_DV, 2 * GLA_DK), F32),
                        pltpu.VMEM((tm, GLA_VWIDTH), F32)],
        compiler_params=pltpu.CompilerParams(
            dimension_semantics=("arbitrary", "arbitrary"), vmem_limit_bytes=VMEM_LIMIT),
        name="gla",
    )(q, k, v_t, la, gate, gain, later, member)


def _ffn_kernel(x_ref, osb_ref, ogla_ref, mod_ref, g2_ref, gfin_ref, wout_ref, wup_ref,
                wconv_ref, bconv_ref, wdown_ref, o_ref, tail_ref, stage_ref, x1_ref):
    tm = x_ref.shape[1]
    d_ff = wdown_ref.shape[0]
    fc = FFN_CHUNK
    n_chunks = d_ff // fc

    @pl.when(pl.program_id(1) == 0)
    def _():
        tail_ref[...] = jnp.zeros_like(tail_ref)

    gate1 = mod_ref[0, 2:3, :]
    shift2 = mod_ref[0, 3:4, :]
    scale2 = mod_ref[0, 4:5, :]
    gain2 = g2_ref[...]

    def chunk_cols(ci):
        return (slice(ci * fc, (ci + 1) * fc), slice(d_ff + ci * fc, d_ff + (ci + 1) * fc))

    def mix(rows):
        mixed = (_dot(osb_ref[0, rows, :], wout_ref[0:SB_WIDTH, :])
                 + _dot(ogla_ref[0, rows, :], wout_ref[SB_WIDTH:SB_WIDTH + GLA_VWIDTH, :]))
        x1 = x_ref[0, rows, :] + (1.0 + gate1) * mixed
        x1_ref[rows, :] = x1
        return (_rms_normalise(x1, gain2) * (1.0 + scale2) + shift2).astype(BF16)

    halves = (slice(0, tm // 2), slice(tm // 2, tm))
    h_parts, first_up = [], []
    for rows in halves:
        hp = mix(rows)
        h_parts.append(hp)
        first_up.append([_dot(hp, wup_ref[:, c]) for c in chunk_cols(0)])
    h = jnp.concatenate(h_parts, axis=0)

    def up(ci):
        return tuple((_dot(h, wup_ref[:, c]), c) for c in chunk_cols(ci))

    def conv(u, cols, slot):
        tail = tail_ref[:, cols]
        slabs = fc // LANES
        for t in range(slabs):
            lanes = slice(t * LANES, (t + 1) * LANES)
            stage_ref[slot * slabs + t, 0:SUBLANES, :] = tail[:, lanes]
            stage_ref[slot * slabs + t, SUBLANES:, :] = u[:, lanes]
        tail_ref[:, cols] = u[tm - SUBLANES:, :]
        out = bconv_ref[:, cols] + wconv_ref[CONV_WIDTH - 1:CONV_WIDTH, cols] * u
        for back in range(1, CONV_WIDTH):
            shifted = jnp.concatenate(
                [stage_ref[slot * slabs + t, pl.ds(SUBLANES - back, tm, stride=1), :]
                 for t in range(slabs)], axis=1)
            out = out + wconv_ref[CONV_WIDTH - 1 - back:CONV_WIDTH - back, cols] * shifted
        return out

    acc = jnp.zeros((tm, o_ref.shape[2]), F32)
    pending = tuple((jnp.concatenate([fu[k] for fu in first_up], axis=0), c)
                    for k, c in enumerate(chunk_cols(0)))
    for ci in range(n_chunks):
        (uv, vcols), (ug, gcols) = pending
        if ci + 1 < n_chunks:
            pending = up(ci + 1)
        val = conv(uv, vcols, 0)
        gte = conv(ug, gcols, 1)
        act = val * (gte * (1.0 / (1.0 + jnp.exp(-gte))))
        acc = acc + _dot(act.astype(BF16), wdown_ref[ci * fc:(ci + 1) * fc, :])

    gate2 = mod_ref[0, 5:6, :]
    y = x1_ref[...] + (1.0 + gate2) * acc
    o_ref[0] = _rms_normalise(y, gfin_ref[...])


def _mix_ffn(x, o_sb, o_gla, mod, g2, g_final, w_o, w_up, w_conv, b_conv, w_down):
    b, s, d = x.shape
    tm = ROW_TILE
    const = lambda bi, i: (0, 0)
    row = lambda bi, i: (bi, i, 0)
    return pl.pallas_call(
        _ffn_kernel,
        out_shape=jax.ShapeDtypeStruct((b, s, d), F32),
        grid=(b, s // tm),
        in_specs=[
            pl.BlockSpec((1, tm, d), row),
            pl.BlockSpec((1, tm, SB_WIDTH), row),
            pl.BlockSpec((1, tm, GLA_VWIDTH), row),
            pl.BlockSpec((1, N_MOD, d), lambda bi, i: (bi, 0, 0)),
            pl.BlockSpec((1, d), const),
            pl.BlockSpec((1, d), const),
            pl.BlockSpec(w_o.shape, const, pipeline_mode=pl.Buffered(1)),
            pl.BlockSpec(w_up.shape, const, pipeline_mode=pl.Buffered(1)),
            pl.BlockSpec(w_conv.shape, const),
            pl.BlockSpec(b_conv.shape, const),
            pl.BlockSpec(w_down.shape, const, pipeline_mode=pl.Buffered(1)),
        ],
        out_specs=pl.BlockSpec((1, tm, d), row),
        scratch_shapes=[pltpu.VMEM((SUBLANES, w_up.shape[1]), F32),
                        pltpu.VMEM((2 * FFN_CHUNK // LANES, SUBLANES + tm, LANES), F32),
                        pltpu.VMEM((tm, d), F32)],
        compiler_params=pltpu.CompilerParams(
            dimension_semantics=("arbitrary", "arbitrary"), vmem_limit_bytes=VMEM_LIMIT),
        name="mix_ffn",
    )(x, o_sb, o_gla, mod, g2, g_final, w_o, w_up, w_conv, b_conv, w_down)


def kernel(x, c, w_ada, b_ada, g_norm1, w_in, w_fg2, b_fg2, g_gla_out, w_out,
           g_norm2, w_up, w_conv, b_conv, w_down, g_final):
    depth = w_ada.shape[0]
    bsz, _, d = x.shape
    gf_col = 3 * SB_WIDTH + 2 * GLA_KWIDTH + 2 * GLA_VWIDTH
    out = x
    for l in range(depth):
        c_pad = jnp.zeros((SUBLANES, d), F32).at[:bsz].set(c)
        mod = _adaln(c_pad, w_ada[l], b_ada[l][None, :])[:bsz].reshape(bsz, N_MOD, d)

        assert w_in.shape[2] == gf_col + GLA_GATE_RANK
        w_gf = jnp.pad(w_in[l][:, gf_col:], ((0, 0), (0, LANES - GLA_GATE_RANK))).astype(BF16)
        w_fg = jnp.pad(w_fg2[l], ((0, LANES - GLA_GATE_RANK), (0, 0)))
        gv_col = 3 * SB_WIDTH + 2 * GLA_KWIDTH
        w_gvt = w_in[l][:, gv_col:gv_col + GLA_VWIDTH].T.astype(BF16)
        sbq, sbk, sbv, gq, gk, gv_t, gg, la, w_o, w_u, w_d = _inproj(
            out, mod, g_norm1[l][None, :], w_in[l].astype(BF16), w_gvt, w_gf, w_fg,
            b_fg2[l][None, :], (w_out[l], w_up[l], w_down[l]))

        o_sb = _sb_attention(sbq, sbk, sbv)
        o_gla = _gla(gq, gk, gv_t, la, gg, g_gla_out[l][None, :])

        assert depth == 1
        out = _mix_ffn(out, o_sb, o_gla, mod, g_norm2[l][None, :], g_final[None, :],
                       w_o, w_u, w_conv[l], b_conv[l][None, :], w_d)
    return out
```

```python
import functools

import jax
import jax.numpy as jnp
from jax import lax
from jax.experimental import pallas as pl
from jax.experimental.pallas import tpu as pltpu

F32 = jnp.float32
BF16 = jnp.bfloat16

EPS = 1e-6
SB_HEAD_DIM = 64
SB_WIDTH = 512
GLA_HEADS = 4
GLA_DK = 64
GLA_DV = 128
GLA_KWIDTH = GLA_HEADS * GLA_DK
GLA_VWIDTH = GLA_HEADS * GLA_DV
GLA_GATE_RANK = 16
GLA_TAU = 16.0
CHUNK = 64
CONV_WIDTH = 3
N_MOD = 6

LANES = 128
SUBLANES = 8
VMEM_LIMIT = 56 * 1024 * 1024

ROW_TILE = 512
GLA_TILE = 1024
ADALN_TILE = 1024
INPROJ_TILE = 1024
SUB_TILE = 256
SB_BLOCK = 128
SB_DROP_ZERO = 150.0
SB_EXP2_CLAMP = 126.0
LOG2_E = 1.4426950408889634
SB_WINDOW = 3
SB_QUERY_TILE = 512
SB_STAGE_LAG = 3
FFN_CHUNK = 256


def _dot(a, b):
    return jnp.dot(a, b, preferred_element_type=F32)


def _dot_nt(a, b):
    return lax.dot_general(a, b, (((1,), (1,)), ((), ())), preferred_element_type=F32)


def _dot_tn(a, b):
    return lax.dot_general(a, b, (((0,), (0,)), ((), ())), preferred_element_type=F32)


def _split_bf16(a):
    hi = a.astype(BF16)
    lo = (a - hi.astype(F32)).astype(BF16)
    return hi, lo


def _rms_normalise(x, gain):
    ms = jnp.mean(x * x, axis=-1, keepdims=True)
    return x * lax.rsqrt(ms + EPS) * gain


def _adaln_kernel(c_ref, w_ref, b_ref, o_ref):
    c = c_ref[...]
    a = c * (1.0 / (1.0 + jnp.exp(-c)))
    a_hi, a_lo = _split_bf16(a)
    w_hi, w_lo = _split_bf16(w_ref[...])
    o_ref[...] = (_dot(a_hi, w_hi) + _dot(a_lo, w_hi) + _dot(a_hi, w_lo)) + b_ref[...]


def _adaln(c_pad, w_ada, b_ada):
    rows, d = c_pad.shape
    n = w_ada.shape[1]
    tn = ADALN_TILE
    return pl.pallas_call(
        _adaln_kernel,
        out_shape=jax.ShapeDtypeStruct((rows, n), F32),
        grid=(n // tn,),
        in_specs=[
            pl.BlockSpec((rows, d), lambda j: (0, 0)),
            pl.BlockSpec((d, tn), lambda j: (0, j)),
            pl.BlockSpec((1, tn), lambda j: (0, j)),
        ],
        out_specs=pl.BlockSpec((rows, tn), lambda j: (0, j)),
        compiler_params=pltpu.CompilerParams(
            dimension_semantics=("arbitrary",), vmem_limit_bytes=VMEM_LIMIT),
        name="adaln",
    )(c_pad, w_ada, b_ada)


def _inproj_kernel(x_ref, mod_ref, g_ref, wint_ref, wfg_ref, bfg_ref,
                   wout_ref, wup_ref, wdown_ref,
                   sbq_ref, sbk_ref, sbv_ref, gq_ref, gk_ref, gvt_ref, gg_ref, la_ref,
                   wout_bf_ref, wup_bf_ref, wdown_bf_ref, win_ref, wgvt_ref, wgf_ref):
    wout_bf_ref[...] = wout_ref[...].astype(BF16)
    wup_bf_ref[...] = wup_ref[...].astype(BF16)
    wdown_bf_ref[...] = wdown_ref[...].astype(BF16)

    w = SB_WIDTH
    kw, vw = GLA_KWIDTH, GLA_VWIDTH
    o = 3 * w
    ov = o + 2 * kw

    @pl.when((pl.program_id(0) == 0) & (pl.program_id(1) == 0))
    def _():
        for c0 in range(0, ov + 2 * vw, w):
            if c0 != ov:
                win_ref[:, c0:c0 + w] = wint_ref[c0:c0 + w, :].T.astype(BF16)
        wgvt_ref[...] = wint_ref[ov:ov + vw, :].astype(BF16)
        gf0 = ov + 2 * vw
        gate_cols = jnp.concatenate(
            [wint_ref[gf0:gf0 + GLA_GATE_RANK, :],
             jnp.zeros((LANES - GLA_GATE_RANK, wint_ref.shape[1]), F32)], axis=0)
        wgf_ref[...] = gate_cols.T.astype(BF16)

    tm = x_ref.shape[1]
    sub = SUB_TILE
    shift = mod_ref[0, 0:1, :]
    scale = mod_ref[0, 1:2, :]
    gain = g_ref[...]
    wfg_hi, wfg_lo = _split_bf16(wfg_ref[...])

    def normalise(si):
        x = x_ref[0, si * sub:(si + 1) * sub, :]
        return (_rms_normalise(x, gain) * (1.0 + scale) + shift).astype(BF16)

    def project(hb, si):
        rows = slice(si * sub, (si + 1) * sub)
        sbq_ref[0, rows, :] = (_dot(hb, win_ref[:, 0:w])
                               * (LOG2_E * SB_HEAD_DIM ** -0.5)).astype(BF16)
        sbk_ref[0, rows, :] = _dot(hb, win_ref[:, w:2 * w]).astype(BF16)
        sbv_ref[0, rows, :] = _dot(hb, win_ref[:, 2 * w:3 * w]).astype(BF16)
        gq_ref[0, rows, :] = (_dot(hb, win_ref[:, o:o + kw]) * (GLA_DK ** -0.5)).astype(BF16)
        gk_ref[0, rows, :] = _dot(hb, win_ref[:, o + kw:o + 2 * kw]).astype(BF16)
        gvt_ref[0, :, rows] = _dot_nt(wgvt_ref[...], hb).astype(BF16)
        gg_ref[0, rows, :] = _dot(hb, win_ref[:, ov + vw:ov + 2 * vw]).astype(BF16)
        gf = _dot(hb, wgf_ref[...])
        gf_hi, gf_lo = _split_bf16(gf)
        pre = (_dot(gf_hi, wfg_hi) + _dot(gf_lo, wfg_hi) + _dot(gf_hi, wfg_lo)) + bfg_ref[...]
        log_sig = jnp.minimum(pre, 0.0) - jnp.log(1.0 + jnp.exp(-jnp.abs(pre)))
        la_ref[0, rows, :] = log_sig * (1.0 / GLA_TAU)

    pending = normalise(0)
    for si in range(tm // sub):
        hb = pending
        if si + 1 < tm // sub:
            pending = normalise(si + 1)
        project(hb, si)


def _inproj(x, mod, g1, w_in_t, w_fg, b_fg, later_weights):
    b, s, d = x.shape
    tm = INPROJ_TILE
    steps = b * (s // tm)
    const = lambda bi, i: (0, 0)
    row = lambda bi, i: (bi, i, 0)
    slab = lambda bi, i: (bi * (s // tm) + i, 0)
    widths = (SB_WIDTH, SB_WIDTH, SB_WIDTH, GLA_KWIDTH, GLA_KWIDTH, None, GLA_VWIDTH)
    out_shape, out_specs = [], []
    for wd in widths:
        if wd is None:
            out_shape.append(jax.ShapeDtypeStruct((b, GLA_VWIDTH, s), BF16))
            out_specs.append(pl.BlockSpec((1, GLA_VWIDTH, tm), lambda bi, i: (bi, 0, i)))
        else:
            out_shape.append(jax.ShapeDtypeStruct((b, s, wd), BF16))
            out_specs.append(pl.BlockSpec((1, tm, wd), row))
    out_shape.append(jax.ShapeDtypeStruct((b, s, GLA_KWIDTH), F32))
    out_specs.append(pl.BlockSpec((1, tm, GLA_KWIDTH), row))
    weight_specs = []
    for wgt in later_weights:
        rows, cols = wgt.shape
        slab_rows = rows // steps
        assert slab_rows * steps == rows and slab_rows % (2 * SUBLANES) == 0
        weight_specs.append(pl.BlockSpec((slab_rows, cols), slab))
        out_shape.append(jax.ShapeDtypeStruct(wgt.shape, BF16))
    out_specs += weight_specs
    return pl.pallas_call(
        _inproj_kernel,
        out_shape=out_shape,
        grid=(b, s // tm),
        in_specs=[
            pl.BlockSpec((1, tm, d), row),
            pl.BlockSpec((1, N_MOD, d), lambda bi, i: (bi, 0, 0)),
            pl.BlockSpec((1, d), const),
            pl.BlockSpec(w_in_t.shape, const, pipeline_mode=pl.Buffered(1)),
            pl.BlockSpec(w_fg.shape, const),
            pl.BlockSpec(b_fg.shape, const),
        ] + weight_specs,
        out_specs=out_specs,
        scratch_shapes=[pltpu.VMEM((d, 3 * SB_WIDTH + 2 * GLA_KWIDTH + 2 * GLA_VWIDTH), BF16),
                        pltpu.VMEM((GLA_VWIDTH, d), BF16),
                        pltpu.VMEM((d, LANES), BF16)],
        compiler_params=pltpu.CompilerParams(
            dimension_semantics=("arbitrary", "arbitrary"), vmem_limit_bytes=VMEM_LIMIT),
        name="inproj",
    )(x, mod, g1, w_in_t, w_fg, b_fg, *later_weights)


def _sb_kernel(q_ref, k_ref, v_ref, o_ref, acc_ref, run_ref):
    blk = SB_BLOCK
    pairs = q_ref.shape[2] // LANES
    lane = lax.broadcasted_iota(jnp.int32, (1, LANES), 1)
    first = lane < SB_HEAD_DIM

    kj = lax.broadcasted_iota(jnp.int32, (2 * blk, 2 * blk), 0) & (blk - 1)
    ks = lax.broadcasted_iota(jnp.int32, (2 * blk, 2 * blk), 1)
    tri = jnp.where((kj >= ks) | (ks >= blk), 1.0, 0.0).astype(BF16)

    t_pos = lax.broadcasted_iota(jnp.int32, (blk, blk), 0)
    s_pos = lax.broadcasted_iota(jnp.int32, (blk, blk), 1)
    causal = s_pos < t_pos

    heads = range(2 * pairs)

    def key_blocks(q_pairs, kbs, fresh):
        starts = [pl.multiple_of(kb * blk, blk) for kb in kbs]
        streams = [(j, p) for j in range(len(kbs)) for p in range(pairs)]
        run = [None if fresh else run_ref[h] for h in heads]
        acc = [None if fresh else acc_ref[:, p * LANES:(p + 1) * LANES] for p in range(pairs)]
        scores, sums = {}, {}

        def stage_scores(j, p):
            k = k_ref[0, pl.ds(starts[j], blk), p * LANES:(p + 1) * LANES]
            zk = jnp.zeros_like(k)
            k_heads = jnp.concatenate([jnp.where(first, k, zk), jnp.where(first, zk, k)], axis=0)
            both = _dot_nt(q_pairs[p], k_heads)
            scores[j, p] = (both[:, :blk], both[:, blk:])

        def stage_sums(j, p):
            out = []
            for z in scores[j, p]:
                d = jnp.maximum(z, jnp.log2(1.0 + jnp.exp2(jnp.minimum(z, SB_EXP2_CLAMP))))
                if fresh and j == 0:
                    d = jnp.where(causal, d, 0.0)
                hi, lo = _split_bf16(d)
                out.append(_dot(jnp.concatenate([hi, lo], axis=1), tri))
            sums[j, p] = out

        def stage_weights(j, p):
            weights = []
            for e, h in enumerate((2 * p, 2 * p + 1)):
                arg = scores[j, p][e] - sums[j, p][e][:, :blk]
                wgt = jnp.exp2(arg if run[h] is None else arg - run[h])
                if fresh and j == 0:
                    wgt = jnp.where(causal, wgt, 0.0)
                total = sums[j, p][e][:, blk:]
                run[h] = total if run[h] is None else run[h] + total
                weights.append(wgt.astype(BF16))
            v = v_ref[0, pl.ds(starts[j], blk), p * LANES:(p + 1) * LANES]
            zv = jnp.zeros_like(v)
            v_heads = jnp.concatenate([jnp.where(first, v, zv), jnp.where(first, zv, v)], axis=0)
            out = _dot(jnp.concatenate(weights, axis=1), v_heads)
            acc[p] = out if acc[p] is None else acc[p] + out

        lag = min(SB_STAGE_LAG, len(streams))
        for t in range(len(streams) + 2 * lag):
            if t < len(streams):
                stage_scores(*streams[t])
            if lag <= t < len(streams) + lag:
                stage_sums(*streams[t - lag])
            if t >= 2 * lag:
                stage_weights(*streams[t - 2 * lag])
        for h in heads:
            run_ref[h] = run[h]
        for p in range(pairs):
            acc_ref[:, p * LANES:(p + 1) * LANES] = acc[p]

    window = SB_WINDOW
    n_sub = q_ref.shape[1] // blk

    def least_drop():
        least = run_ref[0]
        for h in heads[1:]:
            least = jnp.minimum(least, run_ref[h])
        return jnp.min(least)

    def query_block(sub, carry):
        i = pl.program_id(1) * n_sub + sub
        rows = pl.ds(pl.multiple_of(sub * blk, blk), blk)
        q_pairs = [q_ref[0, rows, p * LANES:(p + 1) * LANES] for p in range(pairs)]

        @pl.when(i >= window - 1)
        def _():
            key_blocks(q_pairs, [i - j for j in range(window)], True)

        @pl.when(i < window - 1)
        def _():
            key_blocks(q_pairs, [i], True)

        def cond(state):
            kb, least = state
            return jnp.logical_and(kb >= 0, least < SB_DROP_ZERO)

        def body(state):
            kb, _ = state
            key_blocks(q_pairs, [kb], False)
            return kb - 1, least_drop()

        lax.while_loop(cond, body, (jnp.where(i >= window - 1, i - window, i - 1), least_drop()))
        o_ref[0, rows, :] = acc_ref[...].astype(BF16)
        return carry

    lax.fori_loop(0, n_sub, query_block, 0)


def _sb_attention(q, k, v):
    b, s, w = q.shape
    blk = SB_BLOCK
    heads = w // SB_HEAD_DIM
    kv_spec = pl.BlockSpec((1, s, w), lambda bi, i: (bi, 0, 0))
    tq = SB_QUERY_TILE
    q_spec = pl.BlockSpec((1, tq, w), lambda bi, i: (bi, i, 0))
    return pl.pallas_call(
        _sb_kernel,
        out_shape=jax.ShapeDtypeStruct((b, s, w), BF16),
        grid=(b, s // tq),
        in_specs=[q_spec, kv_spec, kv_spec],
        out_specs=q_spec,
        scratch_shapes=[pltpu.VMEM((blk, w), F32), pltpu.VMEM((heads, blk, blk), F32)],
        compiler_params=pltpu.CompilerParams(
            dimension_semantics=("arbitrary", "arbitrary"),
            vmem_limit_bytes=VMEM_LIMIT),
        name="sb_attn",
    )(q, k, v)


def _gla_kernel(q_ref, k_ref, vt_ref, la_ref, gate_ref, gain_ref, later_ref, member_ref,
                o_ref, state_ref, o_scr):
    tm = q_ref.shape[1]
    n_chunks = tm // CHUNK

    @pl.when(pl.program_id(1) == 0)
    def _():
        state_ref[...] = jnp.zeros_like(state_ref)

    la_hi, la_lo = _split_bf16(la_ref[0])

    later = later_ref[...]
    rev = jnp.concatenate(
        [_dot(later, la_hi[r:r + LANES]) + _dot(later, la_lo[r:r + LANES])
         for r in range(0, tm, LANES)], axis=0)
    k_dec = (k_ref[0].astype(F32) * jnp.exp(rev)).astype(BF16)

    member = member_ref[...]
    decay = jnp.exp(_dot(member, la_hi) + _dot(member, la_lo))

    pairs = GLA_HEADS // 2
    pr = lax.broadcasted_iota(jnp.int32, (2 * GLA_DV, LANES), 0)
    pc = lax.broadcasted_iota(jnp.int32, (2 * GLA_DV, LANES), 1)
    own_head = (pr >= GLA_DV) == (pc >= GLA_DK)

    q = q_ref[0]
    zero_block = jnp.zeros((2 * GLA_DV, LANES), BF16)
    per_tile = LANES // CHUNK
    frame = lax.broadcasted_iota(jnp.int32, (LANES, 1), 0)
    updates = []
    for ci in range(n_chunks):
        tile = slice((ci // per_tile) * LANES, (ci // per_tile + 1) * LANES)
        in_chunk = (frame >= (ci % per_tile) * CHUNK) & (frame < (ci % per_tile + 1) * CHUNK)
        k_tile = k_dec[tile]
        k_tile = jnp.where(in_chunk, k_tile, jnp.zeros_like(k_tile))
        updates.append([
            jnp.where(own_head,
                      _dot(vt_ref[0, p * 2 * GLA_DV:(p + 1) * 2 * GLA_DV, tile],
                           k_tile[:, p * LANES:(p + 1) * LANES]), 0.0)
            for p in range(pairs)])

    states = [state_ref[p] for p in range(pairs)]
    readouts = []
    for ci in range(n_chunks):
        block_rows = []
        for p in range(pairs):
            lanes = slice(p * LANES, (p + 1) * LANES)
            states[p] = decay[ci:ci + 1, lanes] * states[p] + updates[ci][p]
            parts = [zero_block] * pairs
            parts[p] = states[p].astype(BF16)
            block_rows.append(jnp.concatenate(parts, axis=1))
        readouts.append(jnp.concatenate(block_rows, axis=0))
    for p in range(pairs):
        state_ref[p] = states[p]
    for ci in range(n_chunks):
        rows = slice(ci * CHUNK, (ci + 1) * CHUNK)
        o_scr[rows, :] = _dot_nt(q[rows], readouts[ci])

    gain = gain_ref[...]
    gate = gate_ref[0].astype(F32)
    swish = gate * (1.0 / (1.0 + jnp.exp(-gate)))
    for h in range(GLA_HEADS):
        cols = slice(h * GLA_DV, (h + 1) * GLA_DV)
        oh = o_scr[:, cols]
        ms = jnp.mean(oh * oh, axis=-1, keepdims=True)
        o_ref[0, :, cols] = (oh * lax.rsqrt(ms + EPS) * gain[:, cols] * swish[:, cols]).astype(BF16)


def _gla(q, k, v_t, la, gate, gain):
    b, s, _ = q.shape
    tm = GLA_TILE
    row = lambda bi, i: (bi, i, 0)
    const = lambda bi, i: (0, 0)
    n_chunks = tm // CHUNK
    f_idx = jnp.arange(LANES)
    later = ((f_idx[:, None] // CHUNK == f_idx[None, :] // CHUNK)
             & (f_idx[None, :] > f_idx[:, None])).astype(BF16)
    t_idx = jnp.arange(tm)
    member = (jnp.arange(2 * SUBLANES)[:, None] == t_idx[None, :] // CHUNK).astype(BF16)
    assert n_chunks <= 2 * SUBLANES
    return pl.pallas_call(
        _gla_kernel,
        out_shape=jax.ShapeDtypeStruct((b, s, GLA_VWIDTH), BF16),
        grid=(b, s // tm),
        in_specs=[
            pl.BlockSpec((1, tm, GLA_KWIDTH), row),
            pl.BlockSpec((1, tm, GLA_KWIDTH), row),
            pl.BlockSpec((1, GLA_VWIDTH, tm), lambda bi, i: (bi, 0, i)),
            pl.BlockSpec((1, tm, GLA_KWIDTH), row),
            pl.BlockSpec((1, tm, GLA_VWIDTH), row),
            pl.BlockSpec((1, GLA_VWIDTH), const),
            pl.BlockSpec(later.shape, const),
            pl.BlockSpec(member.shape, const),
        ],
        out_specs=pl.BlockSpec((1, tm, GLA_VWIDTH), row),
        scratch_shapes=[pltpu.VMEM((GLA_HEADS // 2, 2 * GLA_DV, 2 * GLA_DK), F32),
                        pltpu.VMEM((tm, GLA_VWIDTH), F32)],
        compiler_params=pltpu.CompilerParams(
            dimension_semantics=("arbitrary", "arbitrary"), vmem_limit_bytes=VMEM_LIMIT),
        name="gla",
    )(q, k, v_t, la, gate, gain, later, member)


def _ffn_kernel(x_ref, osb_ref, ogla_ref, mod_ref, g2_ref, gfin_ref, wout_ref, wup_ref,
                wconv_ref, bconv_ref, wdown_ref, o_ref, tail_ref, stage_ref, x1_ref):
    tm = x_ref.shape[1]
    d_ff = wdown_ref.shape[0]
    fc = FFN_CHUNK
    n_chunks = d_ff // fc

    @pl.when(pl.program_id(1) == 0)
    def _():
        tail_ref[...] = jnp.zeros_like(tail_ref)

    gate1 = mod_ref[0, 2:3, :]
    shift2 = mod_ref[0, 3:4, :]
    scale2 = mod_ref[0, 4:5, :]
    gain2 = g2_ref[...]

    def chunk_cols(ci):
        return (slice(ci * fc, (ci + 1) * fc), slice(d_ff + ci * fc, d_ff + (ci + 1) * fc))

    def mix(rows):
        mixed = (_dot(osb_ref[0, rows, :], wout_ref[0:SB_WIDTH, :])
                 + _dot(ogla_ref[0, rows, :], wout_ref[SB_WIDTH:SB_WIDTH + GLA_VWIDTH, :]))
        x1 = x_ref[0, rows, :] + (1.0 + gate1) * mixed
        x1_ref[rows, :] = x1
        return (_rms_normalise(x1, gain2) * (1.0 + scale2) + shift2).astype(BF16)

    halves = (slice(0, tm // 2), slice(tm // 2, tm))
    h_parts, first_up = [], []
    for rows in halves:
        hp = mix(rows)
        h_parts.append(hp)
        first_up.append([_dot(hp, wup_ref[:, c]) for c in chunk_cols(0)])
    h = jnp.concatenate(h_parts, axis=0)

    def up(ci):
        return tuple((_dot(h, wup_ref[:, c]), c) for c in chunk_cols(ci))

    def conv(u, cols, slot):
        tail = tail_ref[:, cols]
        slabs = fc // LANES
        for t in range(slabs):
            lanes = slice(t * LANES, (t + 1) * LANES)
            stage_ref[slot * slabs + t, 0:SUBLANES, :] = tail[:, lanes]
            stage_ref[slot * slabs + t, SUBLANES:, :] = u[:, lanes]
        tail_ref[:, cols] = u[tm - SUBLANES:, :]
        out = bconv_ref[:, cols] + wconv_ref[CONV_WIDTH - 1:CONV_WIDTH, cols] * u
        for back in range(1, CONV_WIDTH):
            shifted = jnp.concatenate(
                [stage_ref[slot * slabs + t, pl.ds(SUBLANES - back, tm, stride=1), :]
                 for t in range(slabs)], axis=1)
            out = out + wconv_ref[CONV_WIDTH - 1 - back:CONV_WIDTH - back, cols] * shifted
        return out

    acc = jnp.zeros((tm, o_ref.shape[2]), F32)
    pending = tuple((jnp.concatenate([fu[k] for fu in first_up], axis=0), c)
                    for k, c in enumerate(chunk_cols(0)))
    for ci in range(n_chunks):
        (uv, vcols), (ug, gcols) = pending
        if ci + 1 < n_chunks:
            pending = up(ci + 1)
        val = conv(uv, vcols, 0)
        gte = conv(ug, gcols, 1)
        act = val * (gte * (1.0 / (1.0 + jnp.exp(-gte))))
        acc = acc + _dot(act.astype(BF16), wdown_ref[ci * fc:(ci + 1) * fc, :])

    gate2 = mod_ref[0, 5:6, :]
    y = x1_ref[...] + (1.0 + gate2) * acc
    o_ref[0] = _rms_normalise(y, gfin_ref[...])


def _mix_ffn(x, o_sb, o_gla, mod, g2, g_final, w_o, w_up, w_conv, b_conv, w_down):
    b, s, d = x.shape
    tm = ROW_TILE
    const = lambda bi, i: (0, 0)
    row = lambda bi, i: (bi, i, 0)
    return pl.pallas_call(
        _ffn_kernel,
        out_shape=jax.ShapeDtypeStruct((b, s, d), F32),
        grid=(b, s // tm),
        in_specs=[
            pl.BlockSpec((1, tm, d), row),
            pl.BlockSpec((1, tm, SB_WIDTH), row),
            pl.BlockSpec((1, tm, GLA_VWIDTH), row),
            pl.BlockSpec((1, N_MOD, d), lambda bi, i: (bi, 0, 0)),
            pl.BlockSpec((1, d), const),
            pl.BlockSpec((1, d), const),
            pl.BlockSpec(w_o.shape, const, pipeline_mode=pl.Buffered(1)),
            pl.BlockSpec(w_up.shape, const, pipeline_mode=pl.Buffered(1)),
            pl.BlockSpec(w_conv.shape, const),
            pl.BlockSpec(b_conv.shape, const),
            pl.BlockSpec(w_down.shape, const, pipeline_mode=pl.Buffered(1)),
        ],
        out_specs=pl.BlockSpec((1, tm, d), row),
        scratch_shapes=[pltpu.VMEM((SUBLANES, w_up.shape[1]), F32),
                        pltpu.VMEM((2 * FFN_CHUNK // LANES, SUBLANES + tm, LANES), F32),
                        pltpu.VMEM((tm, d), F32)],
        compiler_params=pltpu.CompilerParams(
            dimension_semantics=("arbitrary", "arbitrary"), vmem_limit_bytes=VMEM_LIMIT),
        name="mix_ffn",
    )(x, o_sb, o_gla, mod, g2, g_final, w_o, w_up, w_conv, b_conv, w_down)


def kernel(x, c, w_ada, b_ada, g_norm1, w_in, w_fg2, b_fg2, g_gla_out, w_out,
           g_norm2, w_up, w_conv, b_conv, w_down, g_final):
    depth = w_ada.shape[0]
    bsz, _, d = x.shape
    gf_col = 3 * SB_WIDTH + 2 * GLA_KWIDTH + 2 * GLA_VWIDTH
    out = x
    for l in range(depth):
        c_pad = jnp.zeros((SUBLANES, d), F32).at[:bsz].set(c)
        mod = _adaln(c_pad, w_ada[l], b_ada[l][None, :])[:bsz].reshape(bsz, N_MOD, d)

        assert w_in.shape[2] == gf_col + GLA_GATE_RANK
        w_in_t = jnp.swapaxes(w_in[l], 0, 1)
        w_fg = jnp.pad(w_fg2[l], ((0, LANES - GLA_GATE_RANK), (0, 0)))
        sbq, sbk, sbv, gq, gk, gv_t, gg, la, w_o, w_u, w_d = _inproj(
            out, mod, g_norm1[l][None, :], w_in_t, w_fg,
            b_fg2[l][None, :], (w_out[l], w_up[l], w_down[l]))

        o_sb = _sb_attention(sbq, sbk, sbv)
        o_gla = _gla(gq, gk, gv_t, la, gg, g_gla_out[l][None, :])

        assert depth == 1
        out = _mix_ffn(out, o_sb, o_gla, mod, g_norm2[l][None, :], g_final[None, :],
                       w_o, w_u, w_conv[l], b_conv[l][None, :], w_d)
    return out
```

```python
import functools

import jax
import jax.numpy as jnp
from jax import lax
from jax.experimental import pallas as pl
from jax.experimental.pallas import tpu as pltpu

F32 = jnp.float32
BF16 = jnp.bfloat16

EPS = 1e-6
SB_HEAD_DIM = 64
SB_WIDTH = 512
GLA_HEADS = 4
GLA_DK = 64
GLA_DV = 128
GLA_KWIDTH = GLA_HEADS * GLA_DK
GLA_VWIDTH = GLA_HEADS * GLA_DV
GLA_GATE_RANK = 16
GLA_TAU = 16.0
CHUNK = 64
CONV_WIDTH = 3
N_MOD = 6

LANES = 128
SUBLANES = 8
VMEM_LIMIT = 56 * 1024 * 1024

ROW_TILE = 512
GLA_TILE = 1024
ADALN_TILE = 1024
INPROJ_TILE = 1024
SUB_TILE = 256
SB_BLOCK = 128
SB_DROP_ZERO = 150.0
SB_EXP2_CLAMP = 126.0
LOG2_E = 1.4426950408889634
SB_WINDOW = 3
SB_QUERY_TILE = 512
SB_STAGE_LAG = 3
FFN_CHUNK = 256


def _dot(a, b):
    return jnp.dot(a, b, preferred_element_type=F32)


def _dot_nt(a, b):
    return lax.dot_general(a, b, (((1,), (1,)), ((), ())), preferred_element_type=F32)


def _dot_tn(a, b):
    return lax.dot_general(a, b, (((0,), (0,)), ((), ())), preferred_element_type=F32)


def _split_bf16(a):
    hi = a.astype(BF16)
    lo = (a - hi.astype(F32)).astype(BF16)
    return hi, lo


def _rms_normalise(x, gain):
    ms = jnp.mean(x * x, axis=-1, keepdims=True)
    return x * lax.rsqrt(ms + EPS) * gain


def _adaln_kernel(c_ref, w_ref, b_ref, o_ref):
    c = c_ref[...]
    a = c * (1.0 / (1.0 + jnp.exp(-c)))
    a_hi, a_lo = _split_bf16(a)
    w_hi, w_lo = _split_bf16(w_ref[...])
    o_ref[...] = (_dot(a_hi, w_hi) + _dot(a_lo, w_hi) + _dot(a_hi, w_lo)) + b_ref[...]


def _adaln(c_pad, w_ada, b_ada):
    rows, d = c_pad.shape
    n = w_ada.shape[1]
    tn = ADALN_TILE
    return pl.pallas_call(
        _adaln_kernel,
        out_shape=jax.ShapeDtypeStruct((rows, n), F32),
        grid=(n // tn,),
        in_specs=[
            pl.BlockSpec((rows, d), lambda j: (0, 0)),
            pl.BlockSpec((d, tn), lambda j: (0, j)),
            pl.BlockSpec((1, tn), lambda j: (0, j)),
        ],
        out_specs=pl.BlockSpec((rows, tn), lambda j: (0, j)),
        compiler_params=pltpu.CompilerParams(
            dimension_semantics=("arbitrary",), vmem_limit_bytes=VMEM_LIMIT),
        name="adaln",
    )(c_pad, w_ada, b_ada)


def _inproj_kernel(x_ref, mod_ref, g_ref, wint_ref, wfg_ref, bfg_ref,
                   wout_ref, wup_ref, wdown_ref,
                   sbq_ref, sbk_ref, sbv_ref, gq_ref, gk_ref, gvt_ref, gg_ref, la_ref,
                   wout_bf_ref, wup_bf_ref, wdown_bf_ref, win_ref, wgvt_ref, wgf_ref):
    wout_bf_ref[...] = wout_ref[...].astype(BF16)
    wup_bf_ref[...] = wup_ref[...].astype(BF16)
    wdown_bf_ref[...] = wdown_ref[...].astype(BF16)

    w = SB_WIDTH
    kw, vw = GLA_KWIDTH, GLA_VWIDTH
    o = 3 * w
    ov = o + 2 * kw

    @pl.when((pl.program_id(0) == 0) & (pl.program_id(1) == 0))
    def _():
        for c0 in range(0, ov + 2 * vw, w):
            if c0 != ov:
                win_ref[:, c0:c0 + w] = wint_ref[c0:c0 + w, :].T.astype(BF16)
        wgvt_ref[...] = wint_ref[ov:ov + vw, :].astype(BF16)
        gf0 = ov + 2 * vw
        gate_cols = jnp.concatenate(
            [wint_ref[gf0:gf0 + GLA_GATE_RANK, :],
             jnp.zeros((LANES - GLA_GATE_RANK, wint_ref.shape[1]), F32)], axis=0)
        wgf_ref[...] = gate_cols.T.astype(BF16)

    tm = x_ref.shape[1]
    sub = SUB_TILE
    shift = mod_ref[0, 0:1, :]
    scale = mod_ref[0, 1:2, :]
    gain = g_ref[...]
    wfg_hi, wfg_lo = _split_bf16(wfg_ref[...])

    def normalise(si):
        x = x_ref[0, si * sub:(si + 1) * sub, :]
        return (_rms_normalise(x, gain) * (1.0 + scale) + shift).astype(BF16)

    def project(hb, si):
        rows = slice(si * sub, (si + 1) * sub)
        sbq_ref[0, rows, :] = (_dot(hb, win_ref[:, 0:w])
                               * (LOG2_E * SB_HEAD_DIM ** -0.5)).astype(BF16)
        sbk_ref[0, rows, :] = _dot(hb, win_ref[:, w:2 * w]).astype(BF16)
        sbv_ref[0, rows, :] = _dot(hb, win_ref[:, 2 * w:3 * w]).astype(BF16)
        gq_ref[0, rows, :] = (_dot(hb, win_ref[:, o:o + kw]) * (GLA_DK ** -0.5)).astype(BF16)
        gk_ref[0, rows, :] = _dot(hb, win_ref[:, o + kw:o + 2 * kw]).astype(BF16)
        gvt_ref[0, :, rows] = _dot_nt(wgvt_ref[...], hb).astype(BF16)
        gg_ref[0, rows, :] = _dot(hb, win_ref[:, ov + vw:ov + 2 * vw]).astype(BF16)
        gf = _dot(hb, wgf_ref[...])
        gf_hi, gf_lo = _split_bf16(gf)
        pre = (_dot(gf_hi, wfg_hi) + _dot(gf_lo, wfg_hi) + _dot(gf_hi, wfg_lo)) + bfg_ref[...]
        log_sig = jnp.minimum(pre, 0.0) - jnp.log(1.0 + jnp.exp(-jnp.abs(pre)))
        la_ref[0, rows, :] = log_sig * (1.0 / GLA_TAU)

    pending = normalise(0)
    for si in range(tm // sub):
        hb = pending
        if si + 1 < tm // sub:
            pending = normalise(si + 1)
        project(hb, si)


def _inproj(x, mod, g1, w_in_t, w_fg, b_fg, later_weights):
    b, s, d = x.shape
    tm = INPROJ_TILE
    steps = b * (s // tm)
    const = lambda bi, i: (0, 0)
    row = lambda bi, i: (bi, i, 0)
    slab = lambda bi, i: (bi * (s // tm) + i, 0)
    widths = (SB_WIDTH, SB_WIDTH, SB_WIDTH, GLA_KWIDTH, GLA_KWIDTH, None, GLA_VWIDTH)
    out_shape, out_specs = [], []
    for wd in widths:
        if wd is None:
            out_shape.append(jax.ShapeDtypeStruct((b, GLA_VWIDTH, s), BF16))
            out_specs.append(pl.BlockSpec((1, GLA_VWIDTH, tm), lambda bi, i: (bi, 0, i)))
        else:
            out_shape.append(jax.ShapeDtypeStruct((b, s, wd), BF16))
            out_specs.append(pl.BlockSpec((1, tm, wd), row))
    out_shape.append(jax.ShapeDtypeStruct((b, s, GLA_KWIDTH), F32))
    out_specs.append(pl.BlockSpec((1, tm, GLA_KWIDTH), row))
    weight_specs = []
    for wgt in later_weights:
        rows, cols = wgt.shape
        slab_rows = rows // steps
        assert slab_rows * steps == rows and slab_rows % (2 * SUBLANES) == 0
        weight_specs.append(pl.BlockSpec((slab_rows, cols), slab))
        out_shape.append(jax.ShapeDtypeStruct(wgt.shape, BF16))
    out_specs += weight_specs
    return pl.pallas_call(
        _inproj_kernel,
        out_shape=out_shape,
        grid=(b, s // tm),
        in_specs=[
            pl.BlockSpec((1, tm, d), row),
            pl.BlockSpec((1, N_MOD, d), lambda bi, i: (bi, 0, 0)),
            pl.BlockSpec((1, d), const),
            pl.BlockSpec(w_in_t.shape, const, pipeline_mode=pl.Buffered(1)),
            pl.BlockSpec(w_fg.shape, const),
            pl.BlockSpec(b_fg.shape, const),
        ] + weight_specs,
        out_specs=out_specs,
        scratch_shapes=[pltpu.VMEM((d, 3 * SB_WIDTH + 2 * GLA_KWIDTH + 2 * GLA_VWIDTH), BF16),
                        pltpu.VMEM((GLA_VWIDTH, d), BF16),
                        pltpu.VMEM((d, LANES), BF16)],
        compiler_params=pltpu.CompilerParams(
            dimension_semantics=("arbitrary", "arbitrary"), vmem_limit_bytes=VMEM_LIMIT),
        name="inproj",
    )(x, mod, g1, w_in_t, w_fg, b_fg, *later_weights)


def _sb_kernel(q_ref, k_ref, v_ref, o_ref, acc_ref, run_ref, least_ref):
    blk = SB_BLOCK
    pairs = q_ref.shape[2] // LANES
    lane = lax.broadcasted_iota(jnp.int32, (1, LANES), 1)
    first = lane < SB_HEAD_DIM

    kj = lax.broadcasted_iota(jnp.int32, (2 * blk, 2 * blk), 0) & (blk - 1)
    ks = lax.broadcasted_iota(jnp.int32, (2 * blk, 2 * blk), 1)
    tri = jnp.where((kj >= ks) | (ks >= blk), 1.0, 0.0).astype(BF16)

    t_pos = lax.broadcasted_iota(jnp.int32, (blk, blk), 0)
    s_pos = lax.broadcasted_iota(jnp.int32, (blk, blk), 1)
    causal = s_pos < t_pos

    heads = range(2 * pairs)

    def key_blocks(q_pairs, kbs, fresh):
        starts = [pl.multiple_of(kb * blk, blk) for kb in kbs]
        streams = [(j, p) for j in range(len(kbs)) for p in range(pairs)]
        run = [None if fresh else run_ref[h] for h in heads]
        run_before = {}
        acc = [None if fresh else acc_ref[:, p * LANES:(p + 1) * LANES] for p in range(pairs)]
        scores, sums = {}, {}

        def stage_scores(j, p):
            k = k_ref[0, pl.ds(starts[j], blk), p * LANES:(p + 1) * LANES]
            zk = jnp.zeros_like(k)
            k_heads = jnp.concatenate([jnp.where(first, k, zk), jnp.where(first, zk, k)], axis=0)
            both = _dot_nt(q_pairs[p], k_heads)
            scores[j, p] = (both[:, :blk], both[:, blk:])

        def stage_sums(j, p):
            out = []
            for e, z in enumerate(scores[j, p]):
                d = jnp.maximum(z, jnp.log2(1.0 + jnp.exp2(jnp.minimum(z, SB_EXP2_CLAMP))))
                if fresh and j == 0:
                    d = jnp.where(causal, d, 0.0)
                hi, lo = _split_bf16(d)
                sm = _dot(jnp.concatenate([hi, lo], axis=1), tri)
                out.append(sm)
                h = 2 * p + e
                run_before[j, h] = run[h]
                run[h] = sm[:, blk:] if run[h] is None else run[h] + sm[:, blk:]
            sums[j, p] = out

        def stage_weights(j, p):
            weights = []
            for e, h in enumerate((2 * p, 2 * p + 1)):
                arg = scores[j, p][e] - sums[j, p][e][:, :blk]
                before = run_before[j, h]
                wgt = jnp.exp2(arg if before is None else arg - before)
                if fresh and j == 0:
                    wgt = jnp.where(causal, wgt, 0.0)
                weights.append(wgt.astype(BF16))
            v = v_ref[0, pl.ds(starts[j], blk), p * LANES:(p + 1) * LANES]
            zv = jnp.zeros_like(v)
            v_heads = jnp.concatenate([jnp.where(first, v, zv), jnp.where(first, zv, v)], axis=0)
            out = _dot(jnp.concatenate(weights, axis=1), v_heads)
            acc[p] = out if acc[p] is None else acc[p] + out

        lag = min(SB_STAGE_LAG, len(streams))
        least = None
        for t in range(len(streams) + 2 * lag):
            if t < len(streams):
                stage_scores(*streams[t])
            if lag <= t < len(streams) + lag:
                stage_sums(*streams[t - lag])
            if t == len(streams) + lag - 1:
                least = run[0]
                for h in heads[1:]:
                    least = jnp.minimum(least, run[h])
                least = jnp.min(least)
                for h in heads:
                    run_ref[h] = run[h]
            if t >= 2 * lag:
                stage_weights(*streams[t - 2 * lag])
        for p in range(pairs):
            acc_ref[:, p * LANES:(p + 1) * LANES] = acc[p]
        return least

    window = SB_WINDOW
    n_sub = q_ref.shape[1] // blk

    def query_block(sub, carry):
        i = pl.program_id(1) * n_sub + sub
        rows = pl.ds(pl.multiple_of(sub * blk, blk), blk)
        q_pairs = [q_ref[0, rows, p * LANES:(p + 1) * LANES] for p in range(pairs)]

        @pl.when(i >= window - 1)
        def _():
            least_ref[0] = key_blocks(q_pairs, [i - j for j in range(window)], True)

        @pl.when(i < window - 1)
        def _():
            least_ref[0] = key_blocks(q_pairs, [i], True)

        def cond(state):
            kb, least = state
            return jnp.logical_and(kb >= 0, least < SB_DROP_ZERO)

        def body(state):
            kb, _ = state
            return kb - 1, key_blocks(q_pairs, [kb], False)

        lax.while_loop(cond, body, (jnp.where(i >= window - 1, i - window, i - 1), least_ref[0]))
        o_ref[0, rows, :] = acc_ref[...].astype(BF16)
        return carry

    lax.fori_loop(0, n_sub, query_block, 0)


def _sb_attention(q, k, v):
    b, s, w = q.shape
    blk = SB_BLOCK
    heads = w // SB_HEAD_DIM
    kv_spec = pl.BlockSpec((1, s, w), lambda bi, i: (bi, 0, 0))
    tq = SB_QUERY_TILE
    q_spec = pl.BlockSpec((1, tq, w), lambda bi, i: (bi, i, 0))
    return pl.pallas_call(
        _sb_kernel,
        out_shape=jax.ShapeDtypeStruct((b, s, w), BF16),
        grid=(b, s // tq),
        in_specs=[q_spec, kv_spec, kv_spec],
        out_specs=q_spec,
        scratch_shapes=[pltpu.VMEM((blk, w), F32), pltpu.VMEM((heads, blk, blk), F32),
                        pltpu.SMEM((1,), F32)],
        compiler_params=pltpu.CompilerParams(
            dimension_semantics=("arbitrary", "arbitrary"),
            vmem_limit_bytes=VMEM_LIMIT),
        name="sb_attn",
    )(q, k, v)


def _gla_kernel(q_ref, k_ref, vt_ref, la_ref, gate_ref, gain_ref, later_ref, member_ref,
                o_ref, state_ref, o_scr):
    tm = q_ref.shape[1]
    n_chunks = tm // CHUNK

    @pl.when(pl.program_id(1) == 0)
    def _():
        state_ref[...] = jnp.zeros_like(state_ref)

    la_hi, la_lo = _split_bf16(la_ref[0])

    later = later_ref[...]
    rev = jnp.concatenate(
        [_dot(later, la_hi[r:r + LANES]) + _dot(later, la_lo[r:r + LANES])
         for r in range(0, tm, LANES)], axis=0)
    k_dec = (k_ref[0].astype(F32) * jnp.exp(rev)).astype(BF16)

    member = member_ref[...]
    decay = jnp.exp(_dot(member, la_hi) + _dot(member, la_lo))

    pairs = GLA_HEADS // 2
    pr = lax.broadcasted_iota(jnp.int32, (2 * GLA_DV, LANES), 0)
    pc = lax.broadcasted_iota(jnp.int32, (2 * GLA_DV, LANES), 1)
    own_head = (pr >= GLA_DV) == (pc >= GLA_DK)

    q = q_ref[0]
    zero_block = jnp.zeros((2 * GLA_DV, LANES), BF16)
    per_tile = LANES // CHUNK
    frame = lax.broadcasted_iota(jnp.int32, (LANES, 1), 0)
    updates = []
    for ci in range(n_chunks):
        tile = slice((ci // per_tile) * LANES, (ci // per_tile + 1) * LANES)
        in_chunk = (frame >= (ci % per_tile) * CHUNK) & (frame < (ci % per_tile + 1) * CHUNK)
        k_tile = k_dec[tile]
        k_tile = jnp.where(in_chunk, k_tile, jnp.zeros_like(k_tile))
        updates.append([
            jnp.where(own_head,
                      _dot(vt_ref[0, p * 2 * GLA_DV:(p + 1) * 2 * GLA_DV, tile],
                           k_tile[:, p * LANES:(p + 1) * LANES]), 0.0)
            for p in range(pairs)])

    states = [state_ref[p] for p in range(pairs)]
    readouts = []
    for ci in range(n_chunks):
        block_rows = []
        for p in range(pairs):
            lanes = slice(p * LANES, (p + 1) * LANES)
            states[p] = decay[ci:ci + 1, lanes] * states[p] + updates[ci][p]
            parts = [zero_block] * pairs
            parts[p] = states[p].astype(BF16)
            block_rows.append(jnp.concatenate(parts, axis=1))
        readouts.append(jnp.concatenate(block_rows, axis=0))
    for p in range(pairs):
        state_ref[p] = states[p]
    for ci in range(n_chunks):
        rows = slice(ci * CHUNK, (ci + 1) * CHUNK)
        o_scr[rows, :] = _dot_nt(q[rows], readouts[ci])

    gain = gain_ref[...]
    gate = gate_ref[0].astype(F32)
    swish = gate * (1.0 / (1.0 + jnp.exp(-gate)))
    for h in range(GLA_HEADS):
        cols = slice(h * GLA_DV, (h + 1) * GLA_DV)
        oh = o_scr[:, cols]
        ms = jnp.mean(oh * oh, axis=-1, keepdims=True)
        o_ref[0, :, cols] = (oh * lax.rsqrt(ms + EPS) * gain[:, cols] * swish[:, cols]).astype(BF16)


def _gla(q, k, v_t, la, gate, gain):
    b, s, _ = q.shape
    tm = GLA_TILE
    row = lambda bi, i: (bi, i, 0)
    const = lambda bi, i: (0, 0)
    n_chunks = tm // CHUNK
    f_idx = jnp.arange(LANES)
    later = ((f_idx[:, None] // CHUNK == f_idx[None, :] // CHUNK)
             & (f_idx[None, :] > f_idx[:, None])).astype(BF16)
    t_idx = jnp.arange(tm)
    member = (jnp.arange(2 * SUBLANES)[:, None] == t_idx[None, :] // CHUNK).astype(BF16)
    assert n_chunks <= 2 * SUBLANES
    return pl.pallas_call(
        _gla_kernel,
        out_shape=jax.ShapeDtypeStruct((b, s, GLA_VWIDTH), BF16),
        grid=(b, s // tm),
        in_specs=[
            pl.BlockSpec((1, tm, GLA_KWIDTH), row),
            pl.BlockSpec((1, tm, GLA_KWIDTH), row),
            pl.BlockSpec((1, GLA_VWIDTH, tm), lambda bi, i: (bi, 0, i)),
            pl.BlockSpec((1, tm, GLA_KWIDTH), row),
            pl.BlockSpec((1, tm, GLA_VWIDTH), row),
            pl.BlockSpec((1, GLA_VWIDTH), const),
            pl.BlockSpec(later.shape, const),
            pl.BlockSpec(member.shape, const),
        ],
        out_specs=pl.BlockSpec((1, tm, GLA_VWIDTH), row),
        scratch_shapes=[pltpu.VMEM((GLA_HEADS // 2, 2 * GLA_DV, 2 * GLA_DK), F32),
                        pltpu.VMEM((tm, GLA_VWIDTH), F32)],
        compiler_params=pltpu.CompilerParams(
            dimension_semantics=("arbitrary", "arbitrary"), vmem_limit_bytes=VMEM_LIMIT),
        name="gla",
    )(q, k, v_t, la, gate, gain, later, member)


def _ffn_kernel(x_ref, osb_ref, ogla_ref, mod_ref, g2_ref, gfin_ref, wout_ref, wup_ref,
                wconv_ref, bconv_ref, wdown_ref, o_ref, tail_ref, stage_ref, x1_ref):
    tm = x_ref.shape[1]
    d_ff = wdown_ref.shape[0]
    fc = FFN_CHUNK
    n_chunks = d_ff // fc

    @pl.when(pl.program_id(1) == 0)
    def _():
        tail_ref[...] = jnp.zeros_like(tail_ref)

    gate1 = mod_ref[0, 2:3, :]
    shift2 = mod_ref[0, 3:4, :]
    scale2 = mod_ref[0, 4:5, :]
    gain2 = g2_ref[...]

    def chunk_cols(ci):
        return (slice(ci * fc, (ci + 1) * fc), slice(d_ff + ci * fc, d_ff + (ci + 1) * fc))

    def mix(rows):
        mixed = (_dot(osb_ref[0, rows, :], wout_ref[0:SB_WIDTH, :])
                 + _dot(ogla_ref[0, rows, :], wout_ref[SB_WIDTH:SB_WIDTH + GLA_VWIDTH, :]))
        x1 = x_ref[0, rows, :] + (1.0 + gate1) * mixed
        x1_ref[rows, :] = x1
        return (_rms_normalise(x1, gain2) * (1.0 + scale2) + shift2).astype(BF16)

    halves = (slice(0, tm // 2), slice(tm // 2, tm))
    h_parts, first_up = [], []
    for rows in halves:
        hp = mix(rows)
        h_parts.append(hp)
        first_up.append([_dot(hp, wup_ref[:, c]) for c in chunk_cols(0)])
    h = jnp.concatenate(h_parts, axis=0)

    def up(ci):
        return tuple((_dot(h, wup_ref[:, c]), c) for c in chunk_cols(ci))

    def conv(u, cols, slot):
        tail = tail_ref[:, cols]
        slabs = fc // LANES
        for t in range(slabs):
            lanes = slice(t * LANES, (t + 1) * LANES)
            stage_ref[slot * slabs + t, 0:SUBLANES, :] = tail[:, lanes]
            stage_ref[slot * slabs + t, SUBLANES:, :] = u[:, lanes]
        tail_ref[:, cols] = u[tm - SUBLANES:, :]
        out = bconv_ref[:, cols]
        for back in range(CONV_WIDTH):
            shifted = jnp.concatenate(
                [stage_ref[slot * slabs + t, pl.ds(SUBLANES - back, tm, stride=1), :]
                 for t in range(slabs)], axis=1)
            out = out + wconv_ref[CONV_WIDTH - 1 - back:CONV_WIDTH - back, cols] * shifted
        return out

    acc = jnp.zeros((tm, o_ref.shape[2]), F32)
    pending = tuple((jnp.concatenate([fu[k] for fu in first_up], axis=0), c)
                    for k, c in enumerate(chunk_cols(0)))
    for ci in range(n_chunks):
        (uv, vcols), (ug, gcols) = pending
        if ci + 1 < n_chunks:
            pending = up(ci + 1)
        val = conv(uv, vcols, 0)
        gte = conv(ug, gcols, 1)
        act = val * (gte * (1.0 / (1.0 + jnp.exp(-gte))))
        acc = acc + _dot(act.astype(BF16), wdown_ref[ci * fc:(ci + 1) * fc, :])

    gate2 = mod_ref[0, 5:6, :]
    y = x1_ref[...] + (1.0 + gate2) * acc
    o_ref[0] = _rms_normalise(y, gfin_ref[...])


def _mix_ffn(x, o_sb, o_gla, mod, g2, g_final, w_o, w_up, w_conv, b_conv, w_down):
    b, s, d = x.shape
    tm = ROW_TILE
    const = lambda bi, i: (0, 0)
    row = lambda bi, i: (bi, i, 0)
    return pl.pallas_call(
        _ffn_kernel,
        out_shape=jax.ShapeDtypeStruct((b, s, d), F32),
        grid=(b, s // tm),
        in_specs=[
            pl.BlockSpec((1, tm, d), row),
            pl.BlockSpec((1, tm, SB_WIDTH), row),
            pl.BlockSpec((1, tm, GLA_VWIDTH), row),
            pl.BlockSpec((1, N_MOD, d), lambda bi, i: (bi, 0, 0)),
            pl.BlockSpec((1, d), const),
            pl.BlockSpec((1, d), const),
            pl.BlockSpec(w_o.shape, const, pipeline_mode=pl.Buffered(1)),
            pl.BlockSpec(w_up.shape, const, pipeline_mode=pl.Buffered(1)),
            pl.BlockSpec(w_conv.shape, const),
            pl.BlockSpec(b_conv.shape, const),
            pl.BlockSpec(w_down.shape, const, pipeline_mode=pl.Buffered(1)),
        ],
        out_specs=pl.BlockSpec((1, tm, d), row),
        scratch_shapes=[pltpu.VMEM((SUBLANES, w_up.shape[1]), F32),
                        pltpu.VMEM((2 * FFN_CHUNK // LANES, SUBLANES + tm, LANES), F32),
                        pltpu.VMEM((tm, d), F32)],
        compiler_params=pltpu.CompilerParams(
            dimension_semantics=("arbitrary", "arbitrary"), vmem_limit_bytes=VMEM_LIMIT),
        name="mix_ffn",
    )(x, o_sb, o_gla, mod, g2, g_final, w_o, w_up, w_conv, b_conv, w_down)


def kernel(x, c, w_ada, b_ada, g_norm1, w_in, w_fg2, b_fg2, g_gla_out, w_out,
           g_norm2, w_up, w_conv, b_conv, w_down, g_final):
    depth = w_ada.shape[0]
    bsz, _, d = x.shape
    gf_col = 3 * SB_WIDTH + 2 * GLA_KWIDTH + 2 * GLA_VWIDTH
    out = x
    for l in range(depth):
        c_pad = jnp.zeros((SUBLANES, d), F32).at[:bsz].set(c)
        mod = _adaln(c_pad, w_ada[l], b_ada[l][None, :])[:bsz].reshape(bsz, N_MOD, d)

        assert w_in.shape[2] == gf_col + GLA_GATE_RANK
        w_in_t = jnp.swapaxes(w_in[l], 0, 1)
        w_fg = jnp.pad(w_fg2[l], ((0, LANES - GLA_GATE_RANK), (0, 0)))
        sbq, sbk, sbv, gq, gk, gv_t, gg, la, w_o, w_u, w_d = _inproj(
            out, mod, g_norm1[l][None, :], w_in_t, w_fg,
            b_fg2[l][None, :], (w_out[l], w_up[l], w_down[l]))

        o_sb = _sb_attention(sbq, sbk, sbv)
        o_gla = _gla(gq, gk, gv_t, la, gg, g_gla_out[l][None, :])

        assert depth == 1
        out = _mix_ffn(out, o_sb, o_gla, mod, g_norm2[l][None, :], g_final[None, :],
                       w_o, w_u, w_conv[l], b_conv[l][None, :], w_d)
    return out
```

```python
import functools

import jax
import jax.numpy as jnp
from jax import lax
from jax.experimental import pallas as pl
from jax.experimental.pallas import tpu as pltpu

F32 = jnp.float32
BF16 = jnp.bfloat16

EPS = 1e-6
SB_HEAD_DIM = 64
SB_WIDTH = 512
GLA_HEADS = 4
GLA_DK = 64
GLA_DV = 128
GLA_KWIDTH = GLA_HEADS * GLA_DK
GLA_VWIDTH = GLA_HEADS * GLA_DV
GLA_GATE_RANK = 16
GLA_TAU = 16.0
CHUNK = 64
CONV_WIDTH = 3
N_MOD = 6

LANES = 128
SUBLANES = 8
VMEM_LIMIT = 56 * 1024 * 1024

ROW_TILE = 512
GLA_TILE = 1024
ADALN_TILE = 1024
INPROJ_TILE = 1024
SUB_TILE = 256
SB_BLOCK = 128
SB_DROP_ZERO = 150.0
SB_EXP2_CLAMP = 126.0
LOG2_E = 1.4426950408889634
SB_WINDOW = 3
SB_QUERY_TILE = 512
SB_STAGE_LAG = 3
FFN_CHUNK = 256


def _dot(a, b):
    return jnp.dot(a, b, preferred_element_type=F32)


def _dot_nt(a, b):
    return lax.dot_general(a, b, (((1,), (1,)), ((), ())), preferred_element_type=F32)


def _dot_tn(a, b):
    return lax.dot_general(a, b, (((0,), (0,)), ((), ())), preferred_element_type=F32)


def _split_bf16(a):
    hi = a.astype(BF16)
    lo = (a - hi.astype(F32)).astype(BF16)
    return hi, lo


def _rms_normalise(x, gain):
    ms = jnp.mean(x * x, axis=-1, keepdims=True)
    return x * lax.rsqrt(ms + EPS) * gain


def _adaln_kernel(c_ref, w_ref, b_ref, o_ref):
    c = c_ref[...]
    a = c * (1.0 / (1.0 + jnp.exp(-c)))
    a_hi, a_lo = _split_bf16(a)
    w_hi, w_lo = _split_bf16(w_ref[...])
    o_ref[...] = (_dot(a_hi, w_hi) + _dot(a_lo, w_hi) + _dot(a_hi, w_lo)) + b_ref[...]


def _adaln(c_pad, w_ada, b_ada):
    rows, d = c_pad.shape
    n = w_ada.shape[1]
    tn = ADALN_TILE
    return pl.pallas_call(
        _adaln_kernel,
        out_shape=jax.ShapeDtypeStruct((rows, n), F32),
        grid=(n // tn,),
        in_specs=[
            pl.BlockSpec((rows, d), lambda j: (0, 0)),
            pl.BlockSpec((d, tn), lambda j: (0, j)),
            pl.BlockSpec((1, tn), lambda j: (0, j)),
        ],
        out_specs=pl.BlockSpec((rows, tn), lambda j: (0, j)),
        compiler_params=pltpu.CompilerParams(
            dimension_semantics=("arbitrary",), vmem_limit_bytes=VMEM_LIMIT),
        name="adaln",
    )(c_pad, w_ada, b_ada)


def _inproj_kernel(x_ref, mod_ref, g_ref, wint_ref, wfg_ref, bfg_ref,
                   wout_ref, wup_ref, wdown_ref,
                   sbq_ref, sbk_ref, sbv_ref, gq_ref, gk_ref, gvt_ref, gg_ref, la_ref,
                   wout_bf_ref, wup_bf_ref, wdown_bf_ref, win_ref, wgvt_ref, wgf_ref):
    wout_bf_ref[...] = wout_ref[...].astype(BF16)
    wup_bf_ref[...] = wup_ref[...].astype(BF16)
    wdown_bf_ref[...] = wdown_ref[...].astype(BF16)

    w = SB_WIDTH
    kw, vw = GLA_KWIDTH, GLA_VWIDTH
    o = 3 * w
    ov = o + 2 * kw

    @pl.when((pl.program_id(0) == 0) & (pl.program_id(1) == 0))
    def _():
        for c0 in range(0, ov + 2 * vw, w):
            if c0 != ov:
                win_ref[:, c0:c0 + w] = wint_ref[c0:c0 + w, :].T.astype(BF16)
        wgvt_ref[...] = wint_ref[ov:ov + vw, :].astype(BF16)
        gf0 = ov + 2 * vw
        gate_cols = jnp.concatenate(
            [wint_ref[gf0:gf0 + GLA_GATE_RANK, :],
             jnp.zeros((LANES - GLA_GATE_RANK, wint_ref.shape[1]), F32)], axis=0)
        wgf_ref[...] = gate_cols.T.astype(BF16)

    tm = x_ref.shape[1]
    sub = SUB_TILE
    shift = mod_ref[0, 0:1, :]
    scale = mod_ref[0, 1:2, :]
    gain = g_ref[...]
    wfg_hi, wfg_lo = _split_bf16(wfg_ref[...])

    def normalise(si):
        x = x_ref[0, si * sub:(si + 1) * sub, :]
        return (_rms_normalise(x, gain) * (1.0 + scale) + shift).astype(BF16)

    def project(hb, si):
        rows = slice(si * sub, (si + 1) * sub)
        sbq_ref[0, rows, :] = (_dot(hb, win_ref[:, 0:w])
                               * (LOG2_E * SB_HEAD_DIM ** -0.5)).astype(BF16)
        sbk_ref[0, rows, :] = _dot(hb, win_ref[:, w:2 * w]).astype(BF16)
        sbv_ref[0, rows, :] = _dot(hb, win_ref[:, 2 * w:3 * w]).astype(BF16)
        gq_ref[0, rows, :] = (_dot(hb, win_ref[:, o:o + kw]) * (GLA_DK ** -0.5)).astype(BF16)
        gk_ref[0, rows, :] = _dot(hb, win_ref[:, o + kw:o + 2 * kw]).astype(BF16)
        gvt_ref[0, :, rows] = _dot_nt(wgvt_ref[...], hb).astype(BF16)
        gg_ref[0, rows, :] = _dot(hb, win_ref[:, ov + vw:ov + 2 * vw]).astype(BF16)
        gf = _dot(hb, wgf_ref[...])
        gf_hi, gf_lo = _split_bf16(gf)
        pre = (_dot(gf_hi, wfg_hi) + _dot(gf_lo, wfg_hi) + _dot(gf_hi, wfg_lo)) + bfg_ref[...]
        log_sig = jnp.minimum(pre, 0.0) - jnp.log(1.0 + jnp.exp(-jnp.abs(pre)))
        la_ref[0, rows, :] = log_sig * (1.0 / GLA_TAU)

    pending = normalise(0)
    for si in range(tm // sub):
        hb = pending
        if si + 1 < tm // sub:
            pending = normalise(si + 1)
        project(hb, si)


def _inproj(x, mod, g1, w_in_t, w_fg, b_fg, later_weights):
    b, s, d = x.shape
    tm = INPROJ_TILE
    steps = b * (s // tm)
    const = lambda bi, i: (0, 0)
    row = lambda bi, i: (bi, i, 0)
    slab = lambda bi, i: (bi * (s // tm) + i, 0)
    widths = (SB_WIDTH, SB_WIDTH, SB_WIDTH, GLA_KWIDTH, GLA_KWIDTH, None, GLA_VWIDTH)
    out_shape, out_specs = [], []
    for wd in widths:
        if wd is None:
            out_shape.append(jax.ShapeDtypeStruct((b, GLA_VWIDTH, s), BF16))
            out_specs.append(pl.BlockSpec((1, GLA_VWIDTH, tm), lambda bi, i: (bi, 0, i)))
        else:
            out_shape.append(jax.ShapeDtypeStruct((b, s, wd), BF16))
            out_specs.append(pl.BlockSpec((1, tm, wd), row))
    out_shape.append(jax.ShapeDtypeStruct((b, s, GLA_KWIDTH), F32))
    out_specs.append(pl.BlockSpec((1, tm, GLA_KWIDTH), row))
    weight_specs = []
    for wgt in later_weights:
        rows, cols = wgt.shape
        slab_rows = rows // steps
        assert slab_rows * steps == rows and slab_rows % (2 * SUBLANES) == 0
        weight_specs.append(pl.BlockSpec((slab_rows, cols), slab))
        out_shape.append(jax.ShapeDtypeStruct(wgt.shape, BF16))
    out_specs += weight_specs
    return pl.pallas_call(
        _inproj_kernel,
        out_shape=out_shape,
        grid=(b, s // tm),
        in_specs=[
            pl.BlockSpec((1, tm, d), row),
            pl.BlockSpec((1, N_MOD, d), lambda bi, i: (bi, 0, 0)),
            pl.BlockSpec((1, d), const),
            pl.BlockSpec(w_in_t.shape, const, pipeline_mode=pl.Buffered(1)),
            pl.BlockSpec(w_fg.shape, const),
            pl.BlockSpec(b_fg.shape, const),
        ] + weight_specs,
        out_specs=out_specs,
        scratch_shapes=[pltpu.VMEM((d, 3 * SB_WIDTH + 2 * GLA_KWIDTH + 2 * GLA_VWIDTH), BF16),
                        pltpu.VMEM((GLA_VWIDTH, d), BF16),
                        pltpu.VMEM((d, LANES), BF16)],
        compiler_params=pltpu.CompilerParams(
            dimension_semantics=("arbitrary", "arbitrary"), vmem_limit_bytes=VMEM_LIMIT),
        name="inproj",
    )(x, mod, g1, w_in_t, w_fg, b_fg, *later_weights)


def _sb_kernel(q_ref, k_ref, v_ref, o_ref, acc_ref, run_ref, least_ref):
    blk = SB_BLOCK
    pairs = q_ref.shape[2] // LANES
    lane = lax.broadcasted_iota(jnp.int32, (1, LANES), 1)
    first = lane < SB_HEAD_DIM

    kj = lax.broadcasted_iota(jnp.int32, (2 * blk, 2 * blk), 0) & (blk - 1)
    ks = lax.broadcasted_iota(jnp.int32, (2 * blk, 2 * blk), 1)
    tri = jnp.where((kj >= ks) | (ks >= blk), 1.0, 0.0).astype(BF16)

    t_pos = lax.broadcasted_iota(jnp.int32, (blk, blk), 0)
    s_pos = lax.broadcasted_iota(jnp.int32, (blk, blk), 1)
    causal = s_pos < t_pos

    heads = range(2 * pairs)

    def key_blocks(q_pairs, kbs, fresh):
        starts = [pl.multiple_of(kb * blk, blk) for kb in kbs]
        streams = [(j, p) for j in range(len(kbs)) for p in range(pairs)]
        run = [None if fresh else run_ref[h] for h in heads]
        run_before = {}
        acc = [None if fresh else acc_ref[:, p * LANES:(p + 1) * LANES] for p in range(pairs)]
        scores, sums = {}, {}

        def stage_scores(j, p):
            k = k_ref[0, pl.ds(starts[j], blk), p * LANES:(p + 1) * LANES]
            zk = jnp.zeros_like(k)
            k_heads = jnp.concatenate([jnp.where(first, k, zk), jnp.where(first, zk, k)], axis=0)
            both = _dot_nt(q_pairs[p], k_heads)
            scores[j, p] = (both[:, :blk], both[:, blk:])

        def stage_sums(j, p):
            out = []
            for e, z in enumerate(scores[j, p]):
                d = jnp.maximum(z, jnp.log2(1.0 + jnp.exp2(jnp.minimum(z, SB_EXP2_CLAMP))))
                if fresh and j == 0:
                    d = jnp.where(causal, d, 0.0)
                hi, lo = _split_bf16(d)
                sm = _dot(jnp.concatenate([hi, lo], axis=1), tri)
                out.append(sm)
                h = 2 * p + e
                run_before[j, h] = run[h]
                run[h] = sm[:, blk:] if run[h] is None else run[h] + sm[:, blk:]
            sums[j, p] = out

        def stage_weights(j, p):
            weights = []
            for e, h in enumerate((2 * p, 2 * p + 1)):
                arg = scores[j, p][e] - sums[j, p][e][:, :blk]
                before = run_before[j, h]
                wgt = jnp.exp2(arg if before is None else arg - before)
                if fresh and j == 0:
                    wgt = jnp.where(causal, wgt, 0.0)
                weights.append(wgt.astype(BF16))
            v = v_ref[0, pl.ds(starts[j], blk), p * LANES:(p + 1) * LANES]
            zv = jnp.zeros_like(v)
            v_heads = jnp.concatenate([jnp.where(first, v, zv), jnp.where(first, zv, v)], axis=0)
            out = _dot(jnp.concatenate(weights, axis=1), v_heads)
            acc[p] = out if acc[p] is None else acc[p] + out

        lag = min(SB_STAGE_LAG, len(streams))
        least = None
        for t in range(len(streams) + 2 * lag):
            if t < len(streams):
                stage_scores(*streams[t])
            if lag <= t < len(streams) + lag:
                stage_sums(*streams[t - lag])
            if t == len(streams) + lag - 1:
                least = run[0]
                for h in heads[1:]:
                    least = jnp.minimum(least, run[h])
                least = jnp.min(least)
                for h in heads:
                    run_ref[h] = run[h]
            if t >= 2 * lag:
                stage_weights(*streams[t - 2 * lag])
        for p in range(pairs):
            acc_ref[:, p * LANES:(p + 1) * LANES] = acc[p]
        return least

    window = SB_WINDOW
    n_sub = q_ref.shape[1] // blk

    def query_block(sub, carry):
        i = pl.program_id(1) * n_sub + sub
        rows = pl.ds(pl.multiple_of(sub * blk, blk), blk)
        q_pairs = [q_ref[0, rows, p * LANES:(p + 1) * LANES] for p in range(pairs)]

        @pl.when(i >= window - 1)
        def _():
            least_ref[0] = key_blocks(q_pairs, [i - j for j in range(window)], True)

        @pl.when(i < window - 1)
        def _():
            least_ref[0] = key_blocks(q_pairs, [i], True)

        def cond(state):
            kb, least = state
            return jnp.logical_and(kb >= 0, least < SB_DROP_ZERO)

        def body(state):
            kb, _ = state
            return kb - 1, key_blocks(q_pairs, [kb], False)

        lax.while_loop(cond, body, (jnp.where(i >= window - 1, i - window, i - 1), least_ref[0]))
        o_ref[0, rows, :] = acc_ref[...].astype(BF16)
        return carry

    lax.fori_loop(0, n_sub, query_block, 0)


def _sb_attention(q, k, v):
    b, s, w = q.shape
    blk = SB_BLOCK
    heads = w // SB_HEAD_DIM
    kv_spec = pl.BlockSpec((1, s, w), lambda bi, i: (bi, 0, 0))
    tq = SB_QUERY_TILE
    q_spec = pl.BlockSpec((1, tq, w), lambda bi, i: (bi, i, 0))
    return pl.pallas_call(
        _sb_kernel,
        out_shape=jax.ShapeDtypeStruct((b, s, w), BF16),
        grid=(b, s // tq),
        in_specs=[q_spec, kv_spec, kv_spec],
        out_specs=q_spec,
        scratch_shapes=[pltpu.VMEM((blk, w), F32), pltpu.VMEM((heads, blk, blk), F32),
                        pltpu.SMEM((1,), F32)],
        compiler_params=pltpu.CompilerParams(
            dimension_semantics=("arbitrary", "arbitrary"),
            vmem_limit_bytes=VMEM_LIMIT),
        name="sb_attn",
    )(q, k, v)


def _gla_kernel(q_ref, k_ref, vt_ref, la_ref, gate_ref, gain_ref, later_ref, member_ref,
                o_ref, state_ref, o_scr):
    tm = q_ref.shape[1]
    n_chunks = tm // CHUNK

    @pl.when(pl.program_id(1) == 0)
    def _():
        state_ref[...] = jnp.zeros_like(state_ref)

    la_hi, la_lo = _split_bf16(la_ref[0])

    later = later_ref[...]
    rev = jnp.concatenate(
        [_dot(later, la_hi[r:r + LANES]) + _dot(later, la_lo[r:r + LANES])
         for r in range(0, tm, LANES)], axis=0)
    k_dec = (k_ref[0].astype(F32) * jnp.exp(rev)).astype(BF16)

    member = member_ref[...]
    decay = jnp.exp(_dot(member, la_hi) + _dot(member, la_lo))

    pairs = GLA_HEADS // 2
    pr = lax.broadcasted_iota(jnp.int32, (2 * GLA_DV, LANES), 0)
    pc = lax.broadcasted_iota(jnp.int32, (2 * GLA_DV, LANES), 1)
    own_head = (pr >= GLA_DV) == (pc >= GLA_DK)

    q = q_ref[0]
    zero_block = jnp.zeros((2 * GLA_DV, LANES), BF16)
    per_tile = LANES // CHUNK
    frame = lax.broadcasted_iota(jnp.int32, (LANES, 1), 0)
    updates = []
    for ci in range(n_chunks):
        tile = slice((ci // per_tile) * LANES, (ci // per_tile + 1) * LANES)
        in_chunk = (frame >= (ci % per_tile) * CHUNK) & (frame < (ci % per_tile + 1) * CHUNK)
        k_tile = k_dec[tile]
        k_tile = jnp.where(in_chunk, k_tile, jnp.zeros_like(k_tile))
        updates.append([
            jnp.where(own_head,
                      _dot(vt_ref[0, p * 2 * GLA_DV:(p + 1) * 2 * GLA_DV, tile],
                           k_tile[:, p * LANES:(p + 1) * LANES]), 0.0)
            for p in range(pairs)])

    states = [state_ref[p] for p in range(pairs)]
    readouts = []
    for ci in range(n_chunks):
        block_rows = []
        for p in range(pairs):
            lanes = slice(p * LANES, (p + 1) * LANES)
            states[p] = decay[ci:ci + 1, lanes] * states[p] + updates[ci][p]
            parts = [zero_block] * pairs
            parts[p] = states[p].astype(BF16)
            block_rows.append(jnp.concatenate(parts, axis=1))
        readouts.append(jnp.concatenate(block_rows, axis=0))
    for p in range(pairs):
        state_ref[p] = states[p]
    for ci in range(n_chunks):
        rows = slice(ci * CHUNK, (ci + 1) * CHUNK)
        o_scr[rows, :] = _dot_nt(q[rows], readouts[ci])

    gain = gain_ref[...]
    gate = gate_ref[0].astype(F32)
    swish = gate * (1.0 / (1.0 + jnp.exp(-gate)))
    for h in range(GLA_HEADS):
        cols = slice(h * GLA_DV, (h + 1) * GLA_DV)
        oh = o_scr[:, cols]
        ms = jnp.mean(oh * oh, axis=-1, keepdims=True)
        o_ref[0, :, cols] = (oh * lax.rsqrt(ms + EPS) * gain[:, cols] * swish[:, cols]).astype(BF16)


def _gla(q, k, v_t, la, gate, gain):
    b, s, _ = q.shape
    tm = GLA_TILE
    row = lambda bi, i: (bi, i, 0)
    const = lambda bi, i: (0, 0)
    n_chunks = tm // CHUNK
    f_idx = jnp.arange(LANES)
    later = ((f_idx[:, None] // CHUNK == f_idx[None, :] // CHUNK)
             & (f_idx[None, :] > f_idx[:, None])).astype(BF16)
    t_idx = jnp.arange(tm)
    member = (jnp.arange(2 * SUBLANES)[:, None] == t_idx[None, :] // CHUNK).astype(BF16)
    assert n_chunks <= 2 * SUBLANES
    return pl.pallas_call(
        _gla_kernel,
        out_shape=jax.ShapeDtypeStruct((b, s, GLA_VWIDTH), BF16),
        grid=(b, s // tm),
        in_specs=[
            pl.BlockSpec((1, tm, GLA_KWIDTH), row),
            pl.BlockSpec((1, tm, GLA_KWIDTH), row),
            pl.BlockSpec((1, GLA_VWIDTH, tm), lambda bi, i: (bi, 0, i)),
            pl.BlockSpec((1, tm, GLA_KWIDTH), row),
            pl.BlockSpec((1, tm, GLA_VWIDTH), row),
            pl.BlockSpec((1, GLA_VWIDTH), const),
            pl.BlockSpec(later.shape, const),
            pl.BlockSpec(member.shape, const),
        ],
        out_specs=pl.BlockSpec((1, tm, GLA_VWIDTH), row),
        scratch_shapes=[pltpu.VMEM((GLA_HEADS // 2, 2 * GLA_DV, 2 * GLA_DK), F32),
                        pltpu.VMEM((tm, GLA_VWIDTH), F32)],
        compiler_params=pltpu.CompilerParams(
            dimension_semantics=("arbitrary", "arbitrary"), vmem_limit_bytes=VMEM_LIMIT),
        name="gla",
    )(q, k, v_t, la, gate, gain, later, member)


def _ffn_kernel(x_ref, osb_ref, ogla_ref, mod_ref, g2_ref, gfin_ref, wout_ref, wup_ref,
                wconv_ref, bconv_ref, wdown_ref, o_ref, tail_ref, stage_ref, x1_ref, act_ref):
    tm = x_ref.shape[1]
    d_ff = wdown_ref.shape[0]
    fc = FFN_CHUNK
    n_chunks = d_ff // fc

    @pl.when(pl.program_id(1) == 0)
    def _():
        tail_ref[...] = jnp.zeros_like(tail_ref)

    gate1 = mod_ref[0, 2:3, :]
    shift2 = mod_ref[0, 3:4, :]
    scale2 = mod_ref[0, 4:5, :]
    gain2 = g2_ref[...]

    def chunk_cols(ci):
        return (slice(ci * fc, (ci + 1) * fc), slice(d_ff + ci * fc, d_ff + (ci + 1) * fc))

    def mix(rows):
        mixed = (_dot(osb_ref[0, rows, :], wout_ref[0:SB_WIDTH, :])
                 + _dot(ogla_ref[0, rows, :], wout_ref[SB_WIDTH:SB_WIDTH + GLA_VWIDTH, :]))
        x1 = x_ref[0, rows, :] + (1.0 + gate1) * mixed
        x1_ref[rows, :] = x1
        return (_rms_normalise(x1, gain2) * (1.0 + scale2) + shift2).astype(BF16)

    halves = (slice(0, tm // 2), slice(tm // 2, tm))
    h_parts, first_up = [], []
    for rows in halves:
        hp = mix(rows)
        h_parts.append(hp)
        first_up.append([_dot(hp, wup_ref[:, c]) for c in chunk_cols(0)])
    h = jnp.concatenate(h_parts, axis=0)

    def up(ci):
        return tuple((_dot(h, wup_ref[:, c]), c) for c in chunk_cols(ci))

    def conv(u, cols, slot):
        tail = tail_ref[:, cols]
        slabs = fc // LANES
        for t in range(slabs):
            lanes = slice(t * LANES, (t + 1) * LANES)
            stage_ref[slot * slabs + t, 0:SUBLANES, :] = tail[:, lanes]
            stage_ref[slot * slabs + t, SUBLANES:, :] = u[:, lanes]
        tail_ref[:, cols] = u[tm - SUBLANES:, :]
        out = bconv_ref[:, cols]
        for back in range(CONV_WIDTH):
            shifted = jnp.concatenate(
                [stage_ref[slot * slabs + t, pl.ds(SUBLANES - back, tm, stride=1), :]
                 for t in range(slabs)], axis=1)
            out = out + wconv_ref[CONV_WIDTH - 1 - back:CONV_WIDTH - back, cols] * shifted
        return out

    pending = tuple((jnp.concatenate([fu[k] for fu in first_up], axis=0), c)
                    for k, c in enumerate(chunk_cols(0)))
    for ci in range(n_chunks):
        (uv, vcols), (ug, gcols) = pending
        if ci + 1 < n_chunks:
            pending = up(ci + 1)
        val = conv(uv, vcols, 0)
        gte = conv(ug, gcols, 1)
        act = val * (gte * (1.0 / (1.0 + jnp.exp(-gte))))
        act_ref[:, ci * fc:(ci + 1) * fc] = act.astype(BF16)
    acc = _dot(act_ref[...], wdown_ref[...])

    gate2 = mod_ref[0, 5:6, :]
    y = x1_ref[...] + (1.0 + gate2) * acc
    o_ref[0] = _rms_normalise(y, gfin_ref[...])


def _mix_ffn(x, o_sb, o_gla, mod, g2, g_final, w_o, w_up, w_conv, b_conv, w_down):
    b, s, d = x.shape
    tm = ROW_TILE
    const = lambda bi, i: (0, 0)
    row = lambda bi, i: (bi, i, 0)
    return pl.pallas_call(
        _ffn_kernel,
        out_shape=jax.ShapeDtypeStruct((b, s, d), F32),
        grid=(b, s // tm),
        in_specs=[
            pl.BlockSpec((1, tm, d), row),
            pl.BlockSpec((1, tm, SB_WIDTH), row),
            pl.BlockSpec((1, tm, GLA_VWIDTH), row),
            pl.BlockSpec((1, N_MOD, d), lambda bi, i: (bi, 0, 0)),
            pl.BlockSpec((1, d), const),
            pl.BlockSpec((1, d), const),
            pl.BlockSpec(w_o.shape, const, pipeline_mode=pl.Buffered(1)),
            pl.BlockSpec(w_up.shape, const, pipeline_mode=pl.Buffered(1)),
            pl.BlockSpec(w_conv.shape, const),
            pl.BlockSpec(b_conv.shape, const),
            pl.BlockSpec(w_down.shape, const, pipeline_mode=pl.Buffered(1)),
        ],
        out_specs=pl.BlockSpec((1, tm, d), row),
        scratch_shapes=[pltpu.VMEM((SUBLANES, w_up.shape[1]), F32),
                        pltpu.VMEM((2 * FFN_CHUNK // LANES, SUBLANES + tm, LANES), F32),
                        pltpu.VMEM((tm, d), F32),
                        pltpu.VMEM((tm, w_down.shape[0]), BF16)],
        compiler_params=pltpu.CompilerParams(
            dimension_semantics=("arbitrary", "arbitrary"), vmem_limit_bytes=VMEM_LIMIT),
        name="mix_ffn",
    )(x, o_sb, o_gla, mod, g2, g_final, w_o, w_up, w_conv, b_conv, w_down)


def kernel(x, c, w_ada, b_ada, g_norm1, w_in, w_fg2, b_fg2, g_gla_out, w_out,
           g_norm2, w_up, w_conv, b_conv, w_down, g_final):
    depth = w_ada.shape[0]
    bsz, _, d = x.shape
    gf_col = 3 * SB_WIDTH + 2 * GLA_KWIDTH + 2 * GLA_VWIDTH
    out = x
    for l in range(depth):
        c_pad = jnp.zeros((SUBLANES, d), F32).at[:bsz].set(c)
        mod = _adaln(c_pad, w_ada[l], b_ada[l][None, :])[:bsz].reshape(bsz, N_MOD, d)

        assert w_in.shape[2] == gf_col + GLA_GATE_RANK
        w_in_t = jnp.swapaxes(w_in[l], 0, 1)
        w_fg = jnp.pad(w_fg2[l], ((0, LANES - GLA_GATE_RANK), (0, 0)))
        sbq, sbk, sbv, gq, gk, gv_t, gg, la, w_o, w_u, w_d = _inproj(
            out, mod, g_norm1[l][None, :], w_in_t, w_fg,
            b_fg2[l][None, :], (w_out[l], w_up[l], w_down[l]))

        o_sb = _sb_attention(sbq, sbk, sbv)
        o_gla = _gla(gq, gk, gv_t, la, gg, g_gla_out[l][None, :])

        assert depth == 1
        out = _mix_ffn(out, o_sb, o_gla, mod, g_norm2[l][None, :], g_final[None, :],
                       w_o, w_u, w_conv[l], b_conv[l][None, :], w_d)
    return out
```
